```python
import math
import jax, jax.numpy as jnp
from jax import lax
import numpy as np

D_MODEL = 1024
BATCH = 16
SEQ = 2048
DEPTH = 4

GLA_HEADS = 4
GLA_DK = 32
GLA_DV = 64
GLA_RANK = 16
GLA_GATE_TEMP = 16.0
DIFF_HEADS = 4
DIFF_HD = 64
DIFF_DV = 2 * DIFF_HD
HGRN_HEADS = 4
HGRN_DK = 64
HGRN_DV = 64
CHUNK = 16
Q_BLOCK = 128
ROPE_THETA = 10000.0
EPS = 1e-6
F_FLOOR = 1e-30

GLA_W = GLA_HEADS * GLA_DV
DIFF_W = DIFF_HEADS * DIFF_DV
HGRN_W = HGRN_HEADS * HGRN_DV
MIX_W = GLA_W + DIFF_W + HGRN_W

IN_SIZES = (
    GLA_HEADS * GLA_DK, GLA_HEADS * GLA_DK, GLA_W, GLA_W, GLA_RANK, GLA_RANK,
    DIFF_HEADS * 2 * DIFF_HD, DIFF_HEADS * 2 * DIFF_HD, DIFF_W, DIFF_W,
    HGRN_HEADS * HGRN_DK, HGRN_HEADS * HGRN_DK, HGRN_HEADS * HGRN_DK, HGRN_W, HGRN_W,
)
IN_W = sum(IN_SIZES)

kernel_name = "hybrid_gla_diffattn_hgrn2_bidir_encoder"


def _offsets():
    offs, run = [], 0
    for s in IN_SIZES[:-1]:
        run += s
        offs.append(run)
    return offs


def rms_norm(x, w):
    xf = x.astype(jnp.float32)
    xf = xf * lax.rsqrt(jnp.mean(xf * xf, axis=-1, keepdims=True) + EPS)
    return xf.astype(x.dtype) * w


def split_heads(t, n_heads):
    b, l, _ = t.shape
    return t.reshape(b, l, n_heads, -1).transpose(0, 2, 1, 3)


def merge_heads(t):
    b, h, l, d = t.shape
    return t.transpose(0, 2, 1, 3).reshape(b, l, h * d)


def rope(x, pos):
    d = x.shape[-1]
    inv_freq = ROPE_THETA ** (-jnp.arange(0, d, 2, dtype=jnp.float32) / d)
    ang = pos.astype(jnp.float32)[:, None] * inv_freq[None, :]
    cos, sin = jnp.cos(ang).astype(x.dtype), jnp.sin(ang).astype(x.dtype)
    x1, x2 = x[..., : d // 2], x[..., d // 2:]
    return jnp.concatenate([x1 * cos - x2 * sin, x2 * cos + x1 * sin], axis=-1)


def chunked_gated_scan(q, k, v, log_g):
    b_, h_, l_, dk = q.shape
    dv = v.shape[-1]
    n = l_ // CHUNK
    r = lambda t: t.reshape(b_, h_, n, CHUNK, t.shape[-1])
    q, k, v, lg = r(q), r(k), r(v), r(log_g)
    bcum = jnp.cumsum(lg.astype(jnp.float32), axis=3)
    mask = jnp.tril(jnp.ones((CHUNK, CHUNK), dtype=bool))[:, :, None]
    diff = bcum[:, :, :, :, None, :] - bcum[:, :, :, None, :, :]
    decay = jnp.where(mask, jnp.exp(jnp.where(mask, diff, 0.0)), 0.0)
    attn = jnp.einsum('bhntk,bhnsk,bhntsk->bhnts', q, k, decay.astype(q.dtype))
    o_intra = jnp.einsum('bhnts,bhnsv->bhntv', attn, v)
    b_last = bcum[:, :, :, -1:, :]
    u = jnp.einsum('bhnsk,bhnsv->bhnkv', k * jnp.exp(b_last - bcum).astype(k.dtype), v)
    g_chunk = jnp.exp(b_last[:, :, :, 0, :]).astype(u.dtype)

    def step(state, inp):
        g_n, u_n = inp
        return g_n[..., None] * state + u_n, state

    s0 = jnp.zeros((b_, h_, dk, dv), u.dtype)
    _, s_prev = lax.scan(step, s0, (jnp.moveaxis(g_chunk, 2, 0), jnp.moveaxis(u, 2, 0)))
    s_prev = jnp.moveaxis(s_prev, 0, 2)
    o_inter = jnp.einsum('bhntk,bhnkv->bhntv', q * jnp.exp(bcum).astype(q.dtype), s_prev)
    return (o_intra + o_inter).reshape(b_, h_, l_, dv).astype(v.dtype)


def bidir_scan(q, k_f, k_b, v, lg_f, lg_b):
    flip = lambda t: jnp.flip(t, axis=2)
    o_f = chunked_gated_scan(q, k_f, v, lg_f)
    o_b = flip(chunked_gated_scan(flip(q), flip(k_b), flip(v), flip(lg_b)))
    return o_f + o_b


def gla_branch(q, k, v, g, a_f, a_b, wa2_f, ba_f, wa2_b, ba_b, norm_w):
    q = split_heads(q, GLA_HEADS) * (GLA_DK ** -0.5)
    k = split_heads(k, GLA_HEADS)
    v = split_heads(v, GLA_HEADS)
    lg_f = split_heads(jax.nn.log_sigmoid((a_f @ wa2_f + ba_f).astype(jnp.float32)) / GLA_GATE_TEMP, GLA_HEADS)
    lg_b = split_heads(jax.nn.log_sigmoid((a_b @ wa2_b + ba_b).astype(jnp.float32)) / GLA_GATE_TEMP, GLA_HEADS)
    o = bidir_scan(q, k, k, v, lg_f, lg_b)
    o = rms_norm(o, norm_w)
    return merge_heads(o) * jax.nn.silu(g)


def diff_branch(q, k, v, g, lq1, lk1, lq2, lk2, norm_w, lambda_init, pos):
    b_, l_, _ = q.shape
    q = q.reshape(b_, l_, DIFF_HEADS, 2, DIFF_HD).transpose(3, 0, 2, 1, 4)
    k = k.reshape(b_, l_, DIFF_HEADS, 2, DIFF_HD).transpose(3, 0, 2, 1, 4)
    q = rope(q, pos) * (DIFF_HD ** -0.5)
    k = rope(k, pos)
    v = split_heads(v, DIFF_HEADS)
    lam = (jnp.exp(jnp.sum(lq1 * lk1).astype(jnp.float32))
           - jnp.exp(jnp.sum(lq2 * lk2).astype(jnp.float32)) + lambda_init)
    nb = l_ // Q_BLOCK
    qb = jnp.moveaxis(q.reshape(2, b_, DIFF_HEADS, nb, Q_BLOCK, DIFF_HD), 3, 0)

    def block(qblk):
        s = jnp.einsum('cbhqd,cbhkd->cbhqk', qblk, k).astype(jnp.float32)
        p = jax.nn.softmax(s, axis=-1)
        w = p[0] - lam * p[1]
        return jnp.einsum('bhqk,bhkv->bhqv', w.astype(v.dtype), v)

    o = lax.map(block, qb)
    o = jnp.moveaxis(o, 0, 2).reshape(b_, DIFF_HEADS, l_, DIFF_DV)
    o = rms_norm(o, norm_w) * (1.0 - lambda_init)
    return merge_heads(o) * jax.nn.silu(g)


def hgrn_branch(q, z_f, z_b, i, g, lb, norm_w):
    lb32 = lb.astype(jnp.float32)

    def gates(z):
        z32 = z.astype(jnp.float32)
        one_minus_f = (1.0 - lb32) * jax.nn.sigmoid(-z32)
        f = lb32 + (1.0 - lb32) * jax.nn.sigmoid(z32)
        log_f = jnp.log(jnp.maximum(f, F_FLOOR))
        return split_heads(log_f, HGRN_HEADS), split_heads(one_minus_f.astype(z.dtype), HGRN_HEADS)

    lg_f, k_f = gates(z_f)
    lg_b, k_b = gates(z_b)
    o = bidir_scan(split_heads(q, HGRN_HEADS), k_f, k_b, split_heads(i, HGRN_HEADS), lg_f, lg_b)
    o = rms_norm(o, norm_w)
    return merge_heads(o) * jax.nn.silu(g)


def setup_inputs(seed: int = 0) -> dict:
    key = jax.random.key(seed)
    ks = jax.random.split(key, 20)
    nrm = lambda k, shape, s: jax.random.normal(k, shape, jnp.float32) * s
    return {
        "x": nrm(ks[0], (BATCH, SEQ, D_MODEL), 1.0),
        "norm_pre": 1.0 + nrm(ks[1], (DEPTH, D_MODEL), 0.05),
        "norm_post": 1.0 + nrm(ks[2], (DEPTH, D_MODEL), 0.05),
        "w_in": nrm(ks[3], (DEPTH, D_MODEL, IN_W), D_MODEL ** -0.5),
        "w_out": nrm(ks[4], (DEPTH, MIX_W, D_MODEL), MIX_W ** -0.5),
        "gla_wa2_fwd": nrm(ks[5], (DEPTH, GLA_RANK, GLA_HEADS * GLA_DK), GLA_RANK ** -0.5),
        "gla_ba_fwd": nrm(ks[6], (DEPTH, GLA_HEADS * GLA_DK), 0.1),
        "gla_wa2_bwd": nrm(ks[7], (DEPTH, GLA_RANK, GLA_HEADS * GLA_DK), GLA_RANK ** -0.5),
        "gla_ba_bwd": nrm(ks[8], (DEPTH, GLA_HEADS * GLA_DK), 0.1),
        "gla_norm": 1.0 + nrm(ks[9], (DEPTH, GLA_DV), 0.05),
        "diff_lq1": nrm(ks[10], (DEPTH, DIFF_HD), 0.1),
        "diff_lk1": nrm(ks[11], (DEPTH, DIFF_HD), 0.1),
        "diff_lq2": nrm(ks[12], (DEPTH, DIFF_HD), 0.1),
        "diff_lk2": nrm(ks[13], (DEPTH, DIFF_HD), 0.1),
        "diff_norm": 1.0 + nrm(ks[14], (DEPTH, DIFF_DV), 0.05),
        "hgrn_lb_logits": nrm(ks[15], (DEPTH, HGRN_HEADS * HGRN_DK), 0.1),
        "hgrn_norm": 1.0 + nrm(ks[16], (DEPTH, HGRN_DV), 0.05),
    }


def reference(x, norm_pre, norm_post, w_in, w_out, gla_wa2_fwd, gla_ba_fwd, gla_wa2_bwd, gla_ba_bwd,
              gla_norm, diff_lq1, diff_lk1, diff_lq2, diff_lk2, diff_norm, hgrn_lb_logits, hgrn_norm):
    pos = jnp.arange(x.shape[1], dtype=jnp.int32)
    p_lb = jax.nn.softmax(hgrn_lb_logits.astype(jnp.float32), axis=0)
    lb_all = jnp.cumsum(p_lb, axis=0) - p_lb[0:1]
    offs = _offsets()
    for layer in range(DEPTH):
        lambda_init = 0.8 - 0.6 * math.exp(-0.3 * layer)
        h = rms_norm(x, norm_pre[layer])
        proj = h @ w_in[layer]
        (gq, gk, gv, gg, gaf, gab, dq, dk, dv, dg, hq, hzf, hzb, hi, hg) = jnp.split(proj, offs, axis=-1)
        y_a = gla_branch(gq, gk, gv, gg, gaf, gab, gla_wa2_fwd[layer], gla_ba_fwd[layer],
                         gla_wa2_bwd[layer], gla_ba_bwd[layer], gla_norm[layer])
        y_b = diff_branch(dq, dk, dv, dg, diff_lq1[layer], diff_lk1[layer], diff_lq2[layer], diff_lk2[layer],
                          diff_norm[layer], lambda_init, pos)
        y_c = hgrn_branch(hq, hzf, hzb, hi, hg, lb_all[layer], hgrn_norm[layer])
        y = jnp.concatenate([y_a, y_b, y_c], axis=-1) @ w_out[layer]
        x = x + rms_norm(y, norm_post[layer])
    return x
```

```python
import functools
import math

import numpy as np
import jax
import jax.numpy as jnp
from jax import lax
from jax.experimental import pallas as pl
from jax.experimental.pallas import tpu as pltpu

F32 = jnp.float32
BF16 = jnp.bfloat16

D_MODEL = 1024
DEPTH = 4
GLA_HEADS, GLA_DK, GLA_DV, GLA_RANK = 4, 32, 64, 16
GLA_GATE_TEMP = 16.0
DIFF_HEADS, DIFF_HD = 4, 64
DIFF_DV = 2 * DIFF_HD
HGRN_HEADS, HGRN_DK, HGRN_DV = 4, 64, 64
ROPE_THETA = 10000.0
EPS = 1e-6
F_FLOOR = 1e-30

GLA_QK_W = GLA_HEADS * GLA_DK
GLA_W = GLA_HEADS * GLA_DV
DIFF_QK_W = DIFF_HEADS * 2 * DIFF_HD
DIFF_W = DIFF_HEADS * DIFF_DV
HGRN_QK_W = HGRN_HEADS * HGRN_DK
HGRN_W = HGRN_HEADS * HGRN_DV
MIX_W = GLA_W + DIFF_W + HGRN_W
IN_W = 2 * GLA_QK_W + 2 * GLA_W + 2 * GLA_RANK + 2 * DIFF_QK_W + 2 * DIFF_W + 3 * HGRN_QK_W + 2 * HGRN_W

LANES = 128
A_PAD = LANES
C_GLA = 0
C_DIFF = C_GLA + 2 * GLA_QK_W + 2 * GLA_W
C_HGRN = C_DIFF + 2 * DIFF_QK_W + 2 * DIFF_W
C_A = C_HGRN + 3 * HGRN_QK_W + 2 * HGRN_W
IN_WP = C_A + A_PAD

VMEM_LIMIT = 56 * 1024 * 1024

ROW_TILE = 256
Q_TILE = 256
SCAN_TILE = 128


def _dot(a, b):
    return jnp.dot(a, b, preferred_element_type=F32)


def _dot_nt(a, b):
    return lax.dot_general(a, b, (((1,), (1,)), ((), ())), preferred_element_type=F32)


def _dot_tn(a, b):
    return lax.dot_general(a, b, (((0,), (0,)), ((), ())), preferred_element_type=F32)


def _split_bf16(x):
    hi = x.astype(BF16)
    lo = (x - hi.astype(F32)).astype(BF16)
    return hi, lo


def _sigmoid(x):
    return 1.0 / (1.0 + jnp.exp(-x))


def _silu(x):
    return x * _sigmoid(x)


def _inproj_kernel(x_ref, nw_ref, w_ref, wa2_ref, ba_ref, cos_ref, sin_ref,
                   aq_ref, ak_ref, av_ref, alg_ref, dq_ref, dk_ref, dv_ref,
                   hq_ref, hz_ref, hi_ref, g_ref):
    x = x_ref[...]
    h = x * lax.rsqrt(jnp.mean(x * x, axis=-1, keepdims=True) + EPS) * nw_ref[...]
    hb = h.astype(BF16)

    def proj(c0, width):
        return _dot(hb, w_ref[:, c0:c0 + width])

    p = proj(C_GLA, 2 * GLA_QK_W + 2 * GLA_W)
    aq_ref[...] = p[:, :GLA_QK_W] * (GLA_DK ** -0.5)
    ak_ref[...] = p[:, GLA_QK_W:2 * GLA_QK_W]
    av_ref[...] = p[:, 2 * GLA_QK_W:2 * GLA_QK_W + GLA_W].astype(BF16)
    g_ref[:, 0:GLA_W] = p[:, 2 * GLA_QK_W + GLA_W:]
    a = proj(C_A, A_PAD)
    zz = _dot(a.astype(BF16), wa2_ref[...]) + ba_ref[...]
    alg_ref[...] = (jnp.minimum(zz, 0.0) - jnp.log1p(jnp.exp(-jnp.abs(zz)))) * (1.0 / GLA_GATE_TEMP)

    cos = cos_ref[...]
    sin = sin_ref[...]
    lane = lax.broadcasted_iota(jnp.int32, (1, LANES), 1)
    first_half = (lane % DIFF_HD) < (DIFF_HD // 2)

    def rope_store(c0, out_ref, scale):
        pq = proj(c0, DIFF_QK_W)
        for j in range(DIFF_QK_W // LANES):
            xs = pq[:, j * LANES:(j + 1) * LANES]
            partner = jnp.where(first_half,
                                pltpu.roll(xs, LANES - DIFF_HD // 2, 1),
                                pltpu.roll(xs, DIFF_HD // 2, 1))
            r = xs * cos + partner * sin
            if scale != 1.0:
                r = r * scale
            out_ref[:, j * LANES:(j + 1) * LANES] = r.astype(BF16)

    rope_store(C_DIFF, dq_ref, DIFF_HD ** -0.5)
    rope_store(C_DIFF + DIFF_QK_W, dk_ref, 1.0)
    p = proj(C_DIFF + 2 * DIFF_QK_W, 2 * DIFF_W)
    dv_ref[...] = p[:, :DIFF_W].astype(BF16)
    g_ref[:, GLA_W:GLA_W + DIFF_W] = p[:, DIFF_W:]

    p = proj(C_HGRN, 3 * HGRN_QK_W + 2 * HGRN_W)
    hq_ref[...] = p[:, :HGRN_QK_W]
    hz_ref[...] = p[:, HGRN_QK_W:3 * HGRN_QK_W]
    hi_ref[...] = p[:, 3 * HGRN_QK_W:3 * HGRN_QK_W + HGRN_W].astype(BF16)
    g_ref[:, GLA_W + DIFF_W:] = p[:, 3 * HGRN_QK_W + HGRN_W:]


def _inproj(xf, nw, w, wa2, ba, cos_t, sin_t, seq_len):
    n = xf.shape[0]
    tm = ROW_TILE
    n_pos_tiles = seq_len // tm
    row = lambda i: (i, 0)
    const = lambda i: (0, 0)
    pos = lambda i: (i % n_pos_tiles, 0)
    widths = [(GLA_QK_W, F32), (GLA_QK_W, F32), (GLA_W, BF16), (2 * GLA_QK_W, F32),
              (DIFF_QK_W, BF16), (DIFF_QK_W, BF16), (DIFF_W, BF16),
              (HGRN_QK_W, F32), (2 * HGRN_QK_W, F32), (HGRN_W, BF16), (MIX_W, F32)]
    return pl.pallas_call(
        _inproj_kernel,
        grid=(n // tm,),
        in_specs=[pl.BlockSpec((tm, D_MODEL), row),
                  pl.BlockSpec((1, D_MODEL), const),
                  pl.BlockSpec((D_MODEL, IN_WP), const),
                  pl.BlockSpec((A_PAD, 2 * GLA_QK_W), const),
                  pl.BlockSpec((1, 2 * GLA_QK_W), const),
                  pl.BlockSpec((tm, LANES), pos),
                  pl.BlockSpec((tm, LANES), pos)],
        out_specs=[pl.BlockSpec((tm, wd), row) for wd, _ in widths],
        out_shape=[jax.ShapeDtypeStruct((n, wd), dt) for wd, dt in widths],
        compiler_params=pltpu.CompilerParams(dimension_semantics=("parallel",),
                                             vmem_limit_bytes=VMEM_LIMIT),
        name="inproj",
    )(xf, nw, w, wa2, ba, cos_t, sin_t)


def _attn_kernel(q_ref, k_ref, v_ref, g_ref, nw_ref, lq1_ref, lk1_ref, lq2_ref, lk2_ref, o_ref,
                 *, lambda_init):
    lam = (jnp.exp(jnp.sum(lq1_ref[...] * lk1_ref[...], axis=-1, keepdims=True))
           - jnp.exp(jnp.sum(lq2_ref[...] * lk2_ref[...], axis=-1, keepdims=True)) + lambda_init)
    q = q_ref[...]
    k = k_ref[...]

    def unnormalised(c):
        s = _dot_nt(q[:, c * DIFF_HD:(c + 1) * DIFF_HD], k[:, c * DIFF_HD:(c + 1) * DIFF_HD])
        e = jnp.exp(s - jnp.max(s, axis=-1, keepdims=True))
        return e, jnp.sum(e, axis=-1, keepdims=True)

    e0, l0 = unnormalised(0)
    e1, l1 = unnormalised(1)
    w = e0 * (1.0 / l0) - e1 * (lam / l1)
    o = _dot(w.astype(BF16), v_ref[...])
    o = o * lax.rsqrt(jnp.mean(o * o, axis=-1, keepdims=True) + EPS) * nw_ref[...] * (1.0 - lambda_init)
    o_ref[...] = (o * _silu(g_ref[...])).astype(BF16)


def _attention(dq, dk, dv, g, nw, lq1, lk1, lq2, lk2, batch, seq_len, lambda_init):
    n = dq.shape[0]
    tq = Q_TILE
    nq = seq_len // tq
    qmap = lambda b, h, i: (b * nq + i, h)
    kvmap = lambda b, h, i: (b, h)
    gmap = lambda b, h, i: (b * nq + i, GLA_W // DIFF_DV + h)
    const = lambda b, h, i: (0, 0)
    small = pl.BlockSpec((1, DIFF_HD), const)
    return pl.pallas_call(
        functools.partial(_attn_kernel, lambda_init=lambda_init),
        grid=(batch, DIFF_HEADS, nq),
        in_specs=[pl.BlockSpec((tq, 2 * DIFF_HD), qmap),
                  pl.BlockSpec((seq_len, 2 * DIFF_HD), kvmap),
                  pl.BlockSpec((seq_len, DIFF_DV), kvmap),
                  pl.BlockSpec((tq, DIFF_DV), gmap),
                  pl.BlockSpec((1, DIFF_DV), const),
                  small, small, small, small],
        out_specs=pl.BlockSpec((tq, DIFF_DV), qmap),
        out_shape=jax.ShapeDtypeStruct((n, DIFF_W), BF16),
        compiler_params=pltpu.CompilerParams(dimension_semantics=("parallel", "parallel", "parallel"),
                                             vmem_limit_bytes=VMEM_LIMIT),
        name="diff_attn",
    )(dq, dk, dv, g, nw, lq1, lk1, lq2, lk2)


def _scan_levels(tile):
    nlev = int(math.log2(tile))
    assert 1 << nlev == tile
    return [tile >> (j + 1) for j in range(nlev)]


def _scan_matrix(tile):
    t = tile
    levels = _scan_levels(t)
    blocks = []
    idx = np.arange(t)
    r = idx[None, :]
    for side in ("target", "source"):
        for m in levels:
            blk = np.zeros((t, 2 * t), np.float32)
            start = (idx // m) * m
            end = start + m - 1
            odd = ((idx // m) % 2) == 1
            if side == "target":
                fwd = (r >= start[:, None]) & (r <= idx[:, None]) & odd[:, None]
                bwd = (r >= idx[:, None]) & (r <= end[:, None]) & ~odd[:, None]
            else:
                fwd = (r > idx[:, None]) & (r <= end[:, None]) & ~odd[:, None]
                bwd = (r >= start[:, None]) & (r < idx[:, None]) & odd[:, None]
            blk[:, :t] = fwd
            blk[:, t:] = bwd
            blocks.append(blk)
    z = np.zeros((t, t), np.float32)
    blocks.append(np.concatenate([(r <= idx[:, None]).astype(np.float32), z], 1))
    blocks.append(np.concatenate([(r > idx[:, None]).astype(np.float32), z], 1))
    blocks.append(np.concatenate([z, (r >= idx[:, None]).astype(np.float32)], 1))
    blocks.append(np.concatenate([z, (r < idx[:, None]).astype(np.float32)], 1))
    tot = np.zeros((16, 2 * t), np.float32)
    tot[:8, :t] = 1.0
    tot[8:, t:] = 1.0
    blocks.append(tot)
    return np.concatenate(blocks, 0)


def _bidir_scan(load_tile, v_ref, g_ref, nw, lmat_ref, bd_ref, y_ref,
                args_scr, o_scr, qb_scr, kb_scr, totb_scr, stf_scr, stb_scr,
                *, seq_len, tile, heads, dk, dv):
    t_ = tile
    levels = _scan_levels(t_)
    nlev = len(levels)
    nt = seq_len // t_
    base = 2 * nlev * t_
    row = lax.broadcasted_iota(jnp.int32, (t_, 1), 0)
    xr = lax.broadcasted_iota(jnp.int32, (t_, t_), 0) ^ lax.broadcasted_iota(jnp.int32, (t_, t_), 1)

    stf_scr[...] = jnp.zeros_like(stf_scr)
    stb_scr[...] = jnp.zeros_like(stb_scr)

    def pass_fwd(i, carry):
        r0 = pl.multiple_of(i * t_, t_)
        rows = pl.ds(r0, t_)
        q, kf, kb, lgf, lgb = load_tile(rows)
        hi, lo = _split_bf16(jnp.concatenate([lgf, lgb], axis=0))
        lmat = lmat_ref[...]
        args_scr[...] = _dot(lmat, hi) + _dot(lmat, lo)
        v = v_ref[rows, :]

        def decayed(x, blk):
            return x * jnp.exp(args_scr[blk * t_:(blk + 1) * t_, :])

        lhs = [decayed(q, j) for j in range(nlev)]
        rhs = [decayed(jnp.where((row & m) == 0, kf, kb), nlev + j) for j, m in enumerate(levels)]
        ksum = kf + kb
        qf = decayed(q, 2 * nlev)
        kfd = decayed(kf, 2 * nlev + 1)
        qb_scr[rows, :] = decayed(q, 2 * nlev + 2)
        kb_scr[rows, :] = decayed(kb, 2 * nlev + 3)
        totf = jnp.exp(args_scr[base + 4 * t_:base + 4 * t_ + 1, :])
        totb_scr[i] = jnp.exp(args_scr[base + 4 * t_ + 8:base + 4 * t_ + 16, :])

        outs = []
        for h in range(heads):
            ks = slice(h * dk, (h + 1) * dk)
            vs = slice(h * dv, (h + 1) * dv)
            a = _dot_nt(lhs[0][:, ks].astype(BF16), rhs[0][:, ks].astype(BF16))
            for j in range(1, nlev):
                a = jnp.where(xr < 2 * levels[j],
                              _dot_nt(lhs[j][:, ks].astype(BF16), rhs[j][:, ks].astype(BF16)), a)
            a = jnp.where(xr == 0, _dot_nt(q[:, ks].astype(BF16), ksum[:, ks].astype(BF16)), a)
            vh = v[:, vs]
            st = stf_scr[h]
            o_h = _dot(a.astype(BF16), vh) + _dot_nt(qf[:, ks].astype(BF16), st.astype(BF16))
            stf_scr[h] = st * totf[:, ks] + _dot_tn(vh, kfd[:, ks].astype(BF16))
            outs.append(o_h)
        o_scr[rows, :] = jnp.concatenate(outs, axis=1)
        return carry

    lax.fori_loop(0, nt, pass_fwd, 0)

    def pass_bwd(n, carry):
        i = nt - 1 - n
        r0 = pl.multiple_of(i * t_, t_)
        rows = pl.ds(r0, t_)
        v = v_ref[rows, :]
        qb = qb_scr[rows, :]
        kbd = kb_scr[rows, :]
        totb = totb_scr[i][0:1, :]
        outs = []
        for h in range(heads):
            ks = slice(h * dk, (h + 1) * dk)
            vs = slice(h * dv, (h + 1) * dv)
            st = stb_scr[h]
            outs.append(_dot_nt(qb[:, ks].astype(BF16), st.astype(BF16)))
            stb_scr[h] = st * totb[:, ks] + _dot_tn(v[:, vs], kbd[:, ks].astype(BF16))
        o = o_scr[rows, :] + jnp.concatenate(outs, axis=1)
        hi, lo = _split_bf16(o * o)
        bd = bd_ref[...]
        ms = _dot(hi, bd) + _dot(lo, bd)
        y = o * lax.rsqrt(ms + EPS) * nw * _silu(g_ref[rows, :])
        y_ref[rows, :] = y.astype(BF16)
        return carry

    lax.fori_loop(0, nt, pass_bwd, 0)


def _gla_kernel(q_ref, k_ref, lg_ref, v_ref, g_ref, nw_ref, lmat_ref, bd_ref, y_ref, *scratch,
                seq_len, tile):
    def load_tile(rows):
        k = k_ref[rows, :]
        lg = lg_ref[rows, :]
        return q_ref[rows, :], k, k, lg[:, :GLA_QK_W], lg[:, GLA_QK_W:]

    nw = jnp.concatenate([nw_ref[...]] * GLA_HEADS, axis=1)
    _bidir_scan(load_tile, v_ref, g_ref, nw, lmat_ref, bd_ref, y_ref, *scratch,
                seq_len=seq_len, tile=tile, heads=GLA_HEADS, dk=GLA_DK, dv=GLA_DV)


def _hgrn_kernel(q_ref, z_ref, lbl_ref, v_ref, g_ref, nw_ref, lmat_ref, bd_ref, y_ref, *scratch,
                 seq_len, tile, layer):
    logits = lbl_ref[...]
    e = jnp.exp(logits - jnp.max(logits, axis=0, keepdims=True))
    p = e / jnp.sum(e, axis=0, keepdims=True)
    lb = jnp.zeros((1, HGRN_QK_W), F32)
    for j in range(1, layer + 1):
        lb = lb + p[j:j + 1, :]

    def gates(z):
        t = jnp.exp(-jnp.abs(z))
        r = 1.0 / (1.0 + t)
        pos = z >= 0.0
        sig = jnp.where(pos, r, t * r)
        sig_neg = jnp.where(pos, t * r, r)
        f = lb + (1.0 - lb) * sig
        return jnp.log(jnp.maximum(f, F_FLOOR)), (1.0 - lb) * sig_neg

    def load_tile(rows):
        z = z_ref[rows, :]
        lgf, kf = gates(z[:, :HGRN_QK_W])
        lgb, kb = gates(z[:, HGRN_QK_W:])
        return q_ref[rows, :], kf, kb, lgf, lgb

    nw = jnp.concatenate([nw_ref[...]] * HGRN_HEADS, axis=1)
    _bidir_scan(load_tile, v_ref, g_ref, nw, lmat_ref, bd_ref, y_ref, *scratch,
                seq_len=seq_len, tile=tile, heads=HGRN_HEADS, dk=HGRN_DK, dv=HGRN_DV)


def _scan_scratch(seq_len, tile, heads, dk, dv, lmat_rows):
    w = heads * dk
    return [pltpu.VMEM((lmat_rows, w), F32),
            pltpu.VMEM((seq_len, heads * dv), F32),
            pltpu.VMEM((seq_len, w), F32),
            pltpu.VMEM((seq_len, w), F32),
            pltpu.VMEM((seq_len // tile, 8, w), F32),
            pltpu.VMEM((heads, dv, dk), F32),
            pltpu.VMEM((heads, dv, dk), F32)]


def _scan_call(kernel_fn, inputs, in_widths, g, g_block, nw, lmat, bd, batch, seq_len, heads, dk, dv, name):
    n = batch * seq_len
    bmap = lambda b: (b, 0)
    const = lambda b: (0, 0)
    in_specs = [pl.BlockSpec((seq_len, wd), bmap) if wd is not None else pl.BlockSpec(arr.shape, const)
                for arr, wd in zip(inputs, in_widths)]
    in_specs += [pl.BlockSpec((seq_len, heads * dv), lambda b: (b, g_block)),
                 pl.BlockSpec(nw.shape, const),
                 pl.BlockSpec(lmat.shape, const),
                 pl.BlockSpec(bd.shape, const)]
    return pl.pallas_call(
        kernel_fn,
        grid=(batch,),
        in_specs=in_specs,
        out_specs=pl.BlockSpec((seq_len, heads * dv), bmap),
        out_shape=jax.ShapeDtypeStruct((n, heads * dv), BF16),
        scratch_shapes=_scan_scratch(seq_len, SCAN_TILE, heads, dk, dv, lmat.shape[0]),
        compiler_params=pltpu.CompilerParams(dimension_semantics=("parallel",),
                                             vmem_limit_bytes=VMEM_LIMIT),
        name=name,
    )(*inputs, g, nw, lmat, bd)


def _outproj_kernel(x_ref, ya_ref, yb_ref, yc_ref, w_ref, nw_ref, o_ref):
    y = (_dot(ya_ref[...], w_ref[0:GLA_W, :])
         + _dot(yb_ref[...], w_ref[GLA_W:GLA_W + DIFF_W, :])
         + _dot(yc_ref[...], w_ref[GLA_W + DIFF_W:, :]))
    y = y * lax.rsqrt(jnp.mean(y * y, axis=-1, keepdims=True) + EPS) * nw_ref[...]
    o_ref[...] = x_ref[...] + y


def _outproj(xf, ya, yb, yc, w, nw):
    n = xf.shape[0]
    tm = ROW_TILE
    row = lambda i: (i, 0)
    const = lambda i: (0, 0)
    return pl.pallas_call(
        _outproj_kernel,
        grid=(n // tm,),
        in_specs=[pl.BlockSpec((tm, D_MODEL), row),
                  pl.BlockSpec((tm, GLA_W), row),
                  pl.BlockSpec((tm, DIFF_W), row),
                  pl.BlockSpec((tm, HGRN_W), row),
                  pl.BlockSpec((MIX_W, D_MODEL), const),
                  pl.BlockSpec((1, D_MODEL), const)],
        out_specs=pl.BlockSpec((tm, D_MODEL), row),
        out_shape=jax.ShapeDtypeStruct((n, D_MODEL), F32),
        compiler_params=pltpu.CompilerParams(dimension_semantics=("parallel",),
                                             vmem_limit_bytes=VMEM_LIMIT),
        name="outproj",
    )(xf, ya, yb, yc, w, nw)


def _block_mean_matrix(heads, dv):
    m = np.kron(np.eye(heads, dtype=np.float32), np.full((dv, dv), 1.0 / dv, np.float32))
    return jnp.asarray(m, BF16)


def kernel(x, norm_pre, norm_post, w_in, w_out, gla_wa2_fwd, gla_ba_fwd, gla_wa2_bwd, gla_ba_bwd, gla_norm,
           diff_lq1, diff_lk1, diff_lq2, diff_lk2, diff_norm, hgrn_lb_logits, hgrn_norm):
    batch, seq_len, d_model = x.shape
    assert d_model == D_MODEL and w_in.shape == (DEPTH, D_MODEL, IN_W)
    assert seq_len % ROW_TILE == 0 and seq_len % Q_TILE == 0 and seq_len % SCAN_TILE == 0
    n = batch * seq_len
    xf = x.reshape(n, D_MODEL)

    a0 = 2 * GLA_QK_W + 2 * GLA_W
    w_perm = jnp.concatenate(
        [w_in[:, :, :a0], w_in[:, :, a0 + 2 * GLA_RANK:], w_in[:, :, a0:a0 + 2 * GLA_RANK],
         jnp.zeros((DEPTH, D_MODEL, A_PAD - 2 * GLA_RANK), w_in.dtype)], axis=-1).astype(BF16)
    wa2 = jnp.zeros((DEPTH, A_PAD, 2 * GLA_QK_W), F32)
    wa2 = wa2.at[:, :GLA_RANK, :GLA_QK_W].set(gla_wa2_fwd)
    wa2 = wa2.at[:, GLA_RANK:2 * GLA_RANK, GLA_QK_W:].set(gla_wa2_bwd).astype(BF16)
    ba = jnp.concatenate([gla_ba_fwd, gla_ba_bwd], axis=-1)
    w_out_b = w_out.astype(BF16)

    inv_freq = ROPE_THETA ** (-jnp.arange(0, DIFF_HD, 2, dtype=F32) / DIFF_HD)
    ang = jnp.arange(seq_len, dtype=jnp.int32).astype(F32)[:, None] * inv_freq[None, :]
    cos_t = jnp.tile(jnp.cos(ang), (1, 2 * LANES // DIFF_HD))
    sin_t = jnp.tile(jnp.concatenate([-jnp.sin(ang), jnp.sin(ang)], axis=-1), (1, LANES // DIFF_HD))

    lmat = jnp.asarray(_scan_matrix(SCAN_TILE), BF16)
    bd_gla = _block_mean_matrix(GLA_HEADS, GLA_DV)
    bd_hgrn = _block_mean_matrix(HGRN_HEADS, HGRN_DV)

    for layer in range(DEPTH):
        lambda_init = 0.8 - 0.6 * math.exp(-0.3 * layer)
        (aq, ak, av, alg, dq, dk, dv, hq, hz, hi, g) = _inproj(
            xf, norm_pre[layer][None, :], w_perm[layer], wa2[layer], ba[layer][None, :], cos_t, sin_t, seq_len)
        yb = _attention(dq, dk, dv, g, diff_norm[layer][None, :], diff_lq1[layer][None, :],
                        diff_lk1[layer][None, :], diff_lq2[layer][None, :], diff_lk2[layer][None, :],
                        batch, seq_len, lambda_init)
        ya = _scan_call(
            functools.partial(_gla_kernel, seq_len=seq_len, tile=SCAN_TILE),
            [aq, ak, alg, av], [GLA_QK_W, GLA_QK_W, 2 * GLA_QK_W, GLA_W],
            g, 0, gla_norm[layer][None, :], lmat, bd_gla, batch, seq_len, GLA_HEADS, GLA_DK, GLA_DV, "gla_scan")
        yc = _scan_call(
            functools.partial(_hgrn_kernel, seq_len=seq_len, tile=SCAN_TILE, layer=layer),
            [hq, hz, hgrn_lb_logits, hi], [HGRN_QK_W, 2 * HGRN_QK_W, None, HGRN_W],
            g, (GLA_W + DIFF_W) // HGRN_W, hgrn_norm[layer][None, :], lmat, bd_hgrn,
            batch, seq_len, HGRN_HEADS, HGRN_DK, HGRN_DV, "hgrn_scan")
        xf = _outproj(xf, ya, yb, yc, w_out_b[layer], norm_post[layer][None, :])
    return xf.reshape(batch, seq_len, D_MODEL)
```

```python
import functools
import math

import numpy as np
import jax
import jax.numpy as jnp
from jax import lax
from jax.experimental import pallas as pl
from jax.experimental.pallas import tpu as pltpu

F32 = jnp.float32
BF16 = jnp.bfloat16

D_MODEL = 1024
DEPTH = 4
GLA_HEADS, GLA_DK, GLA_DV, GLA_RANK = 4, 32, 64, 16
GLA_GATE_TEMP = 16.0
DIFF_HEADS, DIFF_HD = 4, 64
DIFF_DV = 2 * DIFF_HD
HGRN_HEADS, HGRN_DK, HGRN_DV = 4, 64, 64
ROPE_THETA = 10000.0
EPS = 1e-6
F_FLOOR = 1e-30
LOG2E = math.log2(math.e)

GLA_QK_W = GLA_HEADS * GLA_DK
GLA_W = GLA_HEADS * GLA_DV
DIFF_QK_W = DIFF_HEADS * 2 * DIFF_HD
DIFF_W = DIFF_HEADS * DIFF_DV
HGRN_QK_W = HGRN_HEADS * HGRN_DK
HGRN_W = HGRN_HEADS * HGRN_DV
MIX_W = GLA_W + DIFF_W + HGRN_W
IN_W = 2 * GLA_QK_W + 2 * GLA_W + 2 * GLA_RANK + 2 * DIFF_QK_W + 2 * DIFF_W + 3 * HGRN_QK_W + 2 * HGRN_W

LANES = 128
A_PAD = LANES
C_GLA = 0
C_DIFF = C_GLA + 2 * GLA_QK_W + 2 * GLA_W
C_HGRN = C_DIFF + 2 * DIFF_QK_W + 2 * DIFF_W
C_A = C_HGRN + 3 * HGRN_QK_W + 2 * HGRN_W
IN_WP = C_A + A_PAD

VMEM_LIMIT = 56 * 1024 * 1024

ROW_TILE = 256
Q_TILE = 512
SCAN_TILE = 128


def _dot(a, b):
    return jnp.dot(a, b, preferred_element_type=F32)


def _dot_nt(a, b):
    return lax.dot_general(a, b, (((1,), (1,)), ((), ())), preferred_element_type=F32)


def _dot_tn(a, b):
    return lax.dot_general(a, b, (((0,), (0,)), ((), ())), preferred_element_type=F32)


def _split_bf16(x):
    hi = x.astype(BF16)
    lo = (x - hi.astype(F32)).astype(BF16)
    return hi, lo


def _sigmoid(x):
    return 1.0 / (1.0 + jnp.exp(-x))


def _silu(x):
    return x * _sigmoid(x)


def _inproj_kernel(x_ref, nw_ref, w_ref, wa2_ref, ba_ref, cos_ref, sin_ref,
                   aq_ref, ak_ref, av_ref, alg_ref, dq_ref, dk_ref, dv_ref,
                   hq_ref, hz_ref, hi_ref, g_ref):
    x = x_ref[...]
    h = x * lax.rsqrt(jnp.mean(x * x, axis=-1, keepdims=True) + EPS) * nw_ref[...]
    hb = h.astype(BF16)

    def proj(c0, width):
        return _dot(hb, w_ref[:, c0:c0 + width])

    p = proj(C_GLA, 2 * GLA_QK_W + 2 * GLA_W)
    aq_ref[...] = p[:, :GLA_QK_W] * (GLA_DK ** -0.5)
    ak_ref[...] = p[:, GLA_QK_W:2 * GLA_QK_W]
    av_ref[...] = p[:, 2 * GLA_QK_W:2 * GLA_QK_W + GLA_W].astype(BF16)
    g_ref[:, 0:GLA_W] = p[:, 2 * GLA_QK_W + GLA_W:]
    a = proj(C_A, A_PAD)
    zz = _dot(a.astype(BF16), wa2_ref[...]) + ba_ref[...]
    alg_ref[...] = (jnp.minimum(zz, 0.0) - jnp.log1p(jnp.exp(-jnp.abs(zz)))) * (LOG2E / GLA_GATE_TEMP)

    cos = cos_ref[...]
    sin = sin_ref[...]
    lane = lax.broadcasted_iota(jnp.int32, (1, LANES), 1)
    first_half = (lane % DIFF_HD) < (DIFF_HD // 2)

    def rope_store(c0, out_ref, scale):
        pq = proj(c0, DIFF_QK_W)
        for j in range(DIFF_QK_W // LANES):
            xs = pq[:, j * LANES:(j + 1) * LANES]
            partner = jnp.where(first_half,
                                pltpu.roll(xs, LANES - DIFF_HD // 2, 1),
                                pltpu.roll(xs, DIFF_HD // 2, 1))
            r = xs * cos + partner * sin
            if scale != 1.0:
                r = r * scale
            out_ref[:, j * LANES:(j + 1) * LANES] = r.astype(BF16)

    rope_store(C_DIFF, dq_ref, DIFF_HD ** -0.5 * LOG2E)
    rope_store(C_DIFF + DIFF_QK_W, dk_ref, 1.0)
    p = proj(C_DIFF + 2 * DIFF_QK_W, 2 * DIFF_W)
    dv_ref[...] = p[:, :DIFF_W].astype(BF16)
    g_ref[:, GLA_W:GLA_W + DIFF_W] = p[:, DIFF_W:]

    p = proj(C_HGRN, 3 * HGRN_QK_W + 2 * HGRN_W)
    hq_ref[...] = p[:, :HGRN_QK_W]
    hz_ref[...] = p[:, HGRN_QK_W:3 * HGRN_QK_W]
    hi_ref[...] = p[:, 3 * HGRN_QK_W:3 * HGRN_QK_W + HGRN_W].astype(BF16)
    g_ref[:, GLA_W + DIFF_W:] = p[:, 3 * HGRN_QK_W + HGRN_W:]


def _inproj(xf, nw, w, wa2, ba, cos_t, sin_t, seq_len):
    n = xf.shape[0]
    tm = ROW_TILE
    n_pos_tiles = seq_len // tm
    row = lambda i: (i, 0)
    const = lambda i: (0, 0)
    pos = lambda i: (i % n_pos_tiles, 0)
    widths = [(GLA_QK_W, F32), (GLA_QK_W, F32), (GLA_W, BF16), (2 * GLA_QK_W, F32),
              (DIFF_QK_W, BF16), (DIFF_QK_W, BF16), (DIFF_W, BF16),
              (HGRN_QK_W, F32), (2 * HGRN_QK_W, F32), (HGRN_W, BF16), (MIX_W, F32)]
    return pl.pallas_call(
        _inproj_kernel,
        grid=(n // tm,),
        in_specs=[pl.BlockSpec((tm, D_MODEL), row),
                  pl.BlockSpec((1, D_MODEL), const),
                  pl.BlockSpec((D_MODEL, IN_WP), const),
                  pl.BlockSpec((A_PAD, 2 * GLA_QK_W), const),
                  pl.BlockSpec((1, 2 * GLA_QK_W), const),
                  pl.BlockSpec((tm, LANES), pos),
                  pl.BlockSpec((tm, LANES), pos)],
        out_specs=[pl.BlockSpec((tm, wd), row) for wd, _ in widths],
        out_shape=[jax.ShapeDtypeStruct((n, wd), dt) for wd, dt in widths],
        compiler_params=pltpu.CompilerParams(dimension_semantics=("parallel",),
                                             vmem_limit_bytes=VMEM_LIMIT),
        name="inproj",
    )(xf, nw, w, wa2, ba, cos_t, sin_t)


ONES_ROWS = 16
KEY_CHUNK = 512
SCORE_LOOKAHEAD = 2


def _attn_kernel(q_ref, k_ref, v_ref, g_ref, nw_ref, lq1_ref, lk1_ref, lq2_ref, lk2_ref, o_ref, vt_scr,
                 *, lambda_init):
    @pl.when(pl.program_id(2) == 0)
    def _():
        vt_scr[0:DIFF_DV, :] = v_ref[...].astype(F32).T.astype(BF16)
        vt_scr[DIFF_DV:, :] = jnp.ones((ONES_ROWS, vt_scr.shape[1]), BF16)

    lam = (jnp.exp(jnp.sum(lq1_ref[...] * lk1_ref[...], axis=-1, keepdims=True))
           - jnp.exp(jnp.sum(lq2_ref[...] * lk2_ref[...], axis=-1, keepdims=True)) + lambda_init)
    q = q_ref[...]
    lane = lax.broadcasted_iota(jnp.int32, (1, 2 * DIFF_HD), 1)

    qmaps = [jnp.where((lane // DIFF_HD) == c, q, jnp.zeros_like(q)) for c in range(2)]
    seq_len = k_ref.shape[0]
    m_run = [None, None]
    acc = [None, None]
    items = [(j, c) for j in range(seq_len // KEY_CHUNK) for c in range(2)]

    def scores(item):
        j, c = item
        return _dot_nt(k_ref[j * KEY_CHUNK:(j + 1) * KEY_CHUNK, :], qmaps[c])

    ahead = [scores(it) for it in items[:SCORE_LOOKAHEAD]]
    for n, (j, c) in enumerate(items):
        st = ahead.pop(0)
        if n + SCORE_LOOKAHEAD < len(items):
            ahead.append(scores(items[n + SCORE_LOOKAHEAD]))
        m_new = jnp.max(st, axis=0, keepdims=True)
        if j > 0:
            m_new = jnp.maximum(m_run[c], m_new)
        e = jnp.exp2(st - m_new).astype(BF16)
        part = _dot(vt_scr[:, j * KEY_CHUNK:(j + 1) * KEY_CHUNK], e)
        acc[c] = part if j == 0 else acc[c] * jnp.exp2(m_run[c] - m_new) + part
        m_run[c] = m_new
    heads_out = [a[0:DIFF_DV, :] * (1.0 / a[DIFF_DV:DIFF_DV + 1, :]) for a in acc]
    o = (heads_out[0] - lam * heads_out[1]).T
    o = o * lax.rsqrt(jnp.mean(o * o, axis=-1, keepdims=True) + EPS) * nw_ref[...] * (1.0 - lambda_init)
    o_ref[...] = (o * _silu(g_ref[...])).astype(BF16)


def _attention(dq, dk, dv, g, nw, lq1, lk1, lq2, lk2, batch, seq_len, lambda_init):
    n = dq.shape[0]
    tq = Q_TILE
    nq = seq_len // tq
    qmap = lambda b, h, i: (b * nq + i, h)
    kvmap = lambda b, h, i: (b, h)
    gmap = lambda b, h, i: (b * nq + i, GLA_W // DIFF_DV + h)
    const = lambda b, h, i: (0, 0)
    small = pl.BlockSpec((1, DIFF_HD), const)
    return pl.pallas_call(
        functools.partial(_attn_kernel, lambda_init=lambda_init),
        grid=(batch, DIFF_HEADS, nq),
        in_specs=[pl.BlockSpec((tq, 2 * DIFF_HD), qmap),
                  pl.BlockSpec((seq_len, 2 * DIFF_HD), kvmap),
                  pl.BlockSpec((seq_len, DIFF_DV), kvmap),
                  pl.BlockSpec((tq, DIFF_DV), gmap),
                  pl.BlockSpec((1, DIFF_DV), const),
                  small, small, small, small],
        out_specs=pl.BlockSpec((tq, DIFF_DV), qmap),
        out_shape=jax.ShapeDtypeStruct((n, DIFF_W), BF16),
        scratch_shapes=[pltpu.VMEM((DIFF_DV + ONES_ROWS, seq_len), BF16)],
        compiler_params=pltpu.CompilerParams(dimension_semantics=("parallel", "parallel", "arbitrary"),
                                             vmem_limit_bytes=VMEM_LIMIT),
        name="diff_attn",
    )(dq, dk, dv, g, nw, lq1, lk1, lq2, lk2)


def _scan_levels(tile):
    nlev = int(math.log2(tile))
    assert 1 << nlev == tile
    return [tile >> (j + 1) for j in range(nlev)]


def _tri_matrices(tile):
    idx = np.arange(tile)
    tril = (idx[None, :] <= idx[:, None]).astype(np.float32)
    return jnp.asarray(tril, BF16), jnp.asarray(tril.T, BF16)


def _bidir_scan(load_tile, v_ref, g_ref, nw, tril_ref, triu_ref, bd_ref, y_ref,
                pf_scr, sb_scr, o_scr, qb_scr, kb_scr, totb_scr, stf_scr, stb_scr,
                *, seq_len, tile, heads, dk, dv):
    t_ = tile
    levels = _scan_levels(t_)
    nt = seq_len // t_
    w = heads * dk
    n_vt = heads * dv // LANES
    hpv = LANES // dv
    sub8 = lax.broadcasted_iota(jnp.int32, (8, 1), 0)
    lane = lax.broadcasted_iota(jnp.int32, (1, LANES), 1)
    xr = (lax.broadcasted_iota(jnp.int32, (t_, hpv * t_), 0)
          ^ (lax.broadcasted_iota(jnp.int32, (t_, hpv * t_), 1) & (t_ - 1)))
    vrow = lax.broadcasted_iota(jnp.int32, (LANES, LANES), 0)
    kcol = lax.broadcasted_iota(jnp.int32, (LANES, LANES), 1)

    def tile_info(p):
        first = p * hpv * dk
        cols = slice((first // LANES) * LANES, (first // LANES + 1) * LANES)
        off = first % LANES
        kmasks = [(lane >= off + r * dk) & (lane < off + (r + 1) * dk) for r in range(hpv)]
        vmasks = [(lane >= r * dv) & (lane < (r + 1) * dv) for r in range(hpv)]
        valid = (vrow // dv) == ((kcol - off) // dk)
        valid = valid & (kcol >= off) & (kcol < off + hpv * dk)
        return cols, kmasks, vmasks, valid

    def stack_heads(x, masks):
        return jnp.concatenate([jnp.where(mk, x, jnp.zeros_like(x)) for mk in masks], axis=0)

    stf_scr[...] = jnp.zeros_like(stf_scr)
    stb_scr[...] = jnp.zeros_like(stb_scr)

    def pass_fwd(i, carry):
        r0 = pl.multiple_of(i * t_, t_)
        rows = pl.ds(r0, t_)
        q, kf, kb, lgf, lgb = load_tile(rows)
        hi_f, lo_f = _split_bf16(lgf)
        hi_b, lo_b = _split_bf16(lgb)
        pp = _dot(tril_ref[...], jnp.concatenate([hi_f, lo_f], axis=1))
        ss = _dot(triu_ref[...], jnp.concatenate([hi_b, lo_b], axis=1))
        pf = pp[:, :w] + pp[:, w:]
        sb = ss[:, :w] + ss[:, w:]
        pf_scr[...] = pf
        sb_scr[...] = sb
        row_cache = {}

        def brow(scr, r):
            key = (id(scr), r)
            if key not in row_cache:
                row_cache[key] = scr[r:r + 1, :]
            return row_cache[key]

        lhs = [[] for _ in levels]
        rhs = [[] for _ in levels]
        for j in range(t_ // 8):
            r8 = slice(8 * j, 8 * j + 8)
            pf_t, sb_t, q_t, kf_t, kb_t = pf[r8], sb[r8], q[r8], kf[r8], kb[r8]
            for li, m in enumerate(levels):
                if m >= 8:
                    mid = (8 * j // (2 * m)) * 2 * m + m
                    gf, gb = brow(pf_scr, mid - 1), brow(sb_scr, mid)
                    if (8 * j // m) % 2 == 0:
                        lhs[li].append(q_t * jnp.exp2(sb_t - gb))
                        rhs[li].append(kf_t * jnp.exp2(gf - pf_t))
                    else:
                        lhs[li].append(q_t * jnp.exp2(pf_t - gf))
                        rhs[li].append(kb_t * jnp.exp2(gb - sb_t))
                    continue
                odd = (sub8 & m) != 0
                kk = jnp.where(odd, kb_t, kf_t)
                if m == 1:
                    lhs[li].append(q_t * jnp.exp2(jnp.where(odd, lgf[r8], lgb[r8])))
                    rhs[li].append(kk)
                    continue
                if m == 4:
                    gf, gb = brow(pf_scr, 8 * j + 3), brow(sb_scr, 8 * j + 4)
                else:
                    low = sub8 < 4
                    gf = jnp.where(low, brow(pf_scr, 8 * j + 1), brow(pf_scr, 8 * j + 5))
                    gb = jnp.where(low, brow(sb_scr, 8 * j + 2), brow(sb_scr, 8 * j + 6))
                lhs[li].append(q_t * jnp.exp2(jnp.where(odd, pf_t - gf, sb_t - gb)))
                rhs[li].append(kk * jnp.exp2(jnp.where(odd, gb - sb_t, gf - pf_t)))
        lhs = [jnp.concatenate(x, axis=0).astype(BF16) for x in lhs]
        rhs = [jnp.concatenate(x, axis=0).astype(BF16) for x in rhs]
        q16 = q.astype(BF16)
        ksum16 = (kf + kb).astype(BF16)
        totf = brow(pf_scr, t_ - 1)
        totb = brow(sb_scr, 0)
        qf = (q * jnp.exp2(pf)).astype(BF16)
        kfd = (kf * jnp.exp2(totf - pf)).astype(BF16)
        qb_scr[rows, :] = (q * jnp.exp2(sb)).astype(BF16)
        kb_scr[rows, :] = (kb * jnp.exp2(totb - sb)).astype(BF16)
        dec_f = jnp.exp2(totf)
        totb_scr[i] = jnp.broadcast_to(jnp.exp2(totb), (8, w))

        v = v_ref[rows, :]
        outs = []
        for p in range(n_vt):
            cols, kmasks, vmasks, valid = tile_info(p)
            a = _dot_nt(lhs[0][:, cols], stack_heads(rhs[0][:, cols], kmasks))
            for li in range(1, len(levels)):
                a = jnp.where(xr < 2 * levels[li],
                              _dot_nt(lhs[li][:, cols], stack_heads(rhs[li][:, cols], kmasks)), a)
            a = jnp.where(xr == 0, _dot_nt(q16[:, cols], stack_heads(ksum16[:, cols], kmasks)), a)
            v_p = v[:, p * LANES:(p + 1) * LANES]
            st = stf_scr[p]
            outs.append(_dot(a.astype(BF16), stack_heads(v_p, vmasks)) + _dot_nt(qf[:, cols], st.astype(BF16)))
            stf_scr[p] = jnp.where(valid, st * dec_f[:, cols] + _dot_tn(v_p, kfd[:, cols]), 0.0)
        o_scr[rows, :] = jnp.concatenate(outs, axis=1)
        return carry

    lax.fori_loop(0, nt, pass_fwd, 0)

    def pass_bwd(n, carry):
        i = nt - 1 - n
        r0 = pl.multiple_of(i * t_, t_)
        rows = pl.ds(r0, t_)
        v = v_ref[rows, :]
        qb = qb_scr[rows, :]
        kbd = kb_scr[rows, :]
        dec_b = totb_scr[i][0:1, :]
        outs = []
        for p in range(n_vt):
            cols, _, _, valid = tile_info(p)
            st = stb_scr[p]
            outs.append(_dot_nt(qb[:, cols], st.astype(BF16)))
            stb_scr[p] = jnp.where(valid, st * dec_b[:, cols] + _dot_tn(v[:, p * LANES:(p + 1) * LANES],
                                                                       kbd[:, cols]), 0.0)
        o = o_scr[rows, :] + jnp.concatenate(outs, axis=1)
        hi, lo = _split_bf16(o * o)
        bd = bd_ref[...]
        ms = _dot(hi, bd) + _dot(lo, bd)
        y = o * lax.rsqrt(ms + EPS) * nw * _silu(g_ref[rows, :])
        y_ref[rows, :] = y.astype(BF16)
        return carry

    lax.fori_loop(0, nt, pass_bwd, 0, unroll=2)


def _gla_kernel(q_ref, k_ref, lg_ref, v_ref, g_ref, nw_ref, tril_ref, triu_ref, bd_ref, y_ref, *scratch,
                seq_len, tile):
    def load_tile(rows):
        k = k_ref[rows, :]
        lg = lg_ref[rows, :]
        return q_ref[rows, :], k, k, lg[:, :GLA_QK_W], lg[:, GLA_QK_W:]

    nw = jnp.concatenate([nw_ref[...]] * GLA_HEADS, axis=1)
    _bidir_scan(load_tile, v_ref, g_ref, nw, tril_ref, triu_ref, bd_ref, y_ref, *scratch,
                seq_len=seq_len, tile=tile, heads=GLA_HEADS, dk=GLA_DK, dv=GLA_DV)


def _hgrn_kernel(q_ref, z_ref, lbl_ref, v_ref, g_ref, nw_ref, tril_ref, triu_ref, bd_ref, y_ref, *scratch,
                 seq_len, tile, layer):
    logits = lbl_ref[...]
    e = jnp.exp(logits - jnp.max(logits, axis=0, keepdims=True))
    p = e / jnp.sum(e, axis=0, keepdims=True)
    lb = jnp.zeros((1, HGRN_QK_W), F32)
    for j in range(1, layer + 1):
        lb = lb + p[j:j + 1, :]

    def gates(z):
        t = jnp.exp(-jnp.abs(z))
        r = 1.0 / (1.0 + t)
        pos = z >= 0.0
        sig = jnp.where(pos, r, t * r)
        sig_neg = jnp.where(pos, t * r, r)
        f = lb + (1.0 - lb) * sig
        return jnp.log(jnp.maximum(f, F_FLOOR)) * LOG2E, (1.0 - lb) * sig_neg

    def load_tile(rows):
        z = z_ref[rows, :]
        lgf, kf = gates(z[:, :HGRN_QK_W])
        lgb, kb = gates(z[:, HGRN_QK_W:])
        return q_ref[rows, :], kf, kb, lgf, lgb

    nw = jnp.concatenate([nw_ref[...]] * HGRN_HEADS, axis=1)
    _bidir_scan(load_tile, v_ref, g_ref, nw, tril_ref, triu_ref, bd_ref, y_ref, *scratch,
                seq_len=seq_len, tile=tile, heads=HGRN_HEADS, dk=HGRN_DK, dv=HGRN_DV)


def _scan_scratch(seq_len, tile, heads, dk, dv):
    w = heads * dk
    n_vt = heads * dv // LANES
    return [pltpu.VMEM((tile, w), F32),
            pltpu.VMEM((tile, w), F32),
            pltpu.VMEM((seq_len, heads * dv), F32),
            pltpu.VMEM((seq_len, w), BF16),
            pltpu.VMEM((seq_len, w), BF16),
            pltpu.VMEM((seq_len // tile, 8, w), F32),
            pltpu.VMEM((n_vt, LANES, LANES), F32),
            pltpu.VMEM((n_vt, LANES, LANES), F32)]


def _scan_call(kernel_fn, inputs, in_widths, g, g_block, nw, tri, bd, batch, seq_len, heads, dk, dv, name):
    n = batch * seq_len
    bmap = lambda b: (b, 0)
    const = lambda b: (0, 0)
    in_specs = [pl.BlockSpec((seq_len, wd), bmap) if wd is not None else pl.BlockSpec(arr.shape, const)
                for arr, wd in zip(inputs, in_widths)]
    in_specs += [pl.BlockSpec((seq_len, heads * dv), lambda b: (b, g_block)),
                 pl.BlockSpec(nw.shape, const),
                 pl.BlockSpec(tri[0].shape, const),
                 pl.BlockSpec(tri[1].shape, const),
                 pl.BlockSpec(bd.shape, const)]
    return pl.pallas_call(
        kernel_fn,
        grid=(batch,),
        in_specs=in_specs,
        out_specs=pl.BlockSpec((seq_len, heads * dv), bmap),
        out_shape=jax.ShapeDtypeStruct((n, heads * dv), BF16),
        scratch_shapes=_scan_scratch(seq_len, SCAN_TILE, heads, dk, dv),
        compiler_params=pltpu.CompilerParams(dimension_semantics=("parallel",),
                                             vmem_limit_bytes=VMEM_LIMIT),
        name=name,
    )(*inputs, g, nw, tri[0], tri[1], bd)


def _outproj_kernel(x_ref, ya_ref, yb_ref, yc_ref, w_ref, nw_ref, o_ref):
    y = (_dot(ya_ref[...], w_ref[0:GLA_W, :])
         + _dot(yb_ref[...], w_ref[GLA_W:GLA_W + DIFF_W, :])
         + _dot(yc_ref[...], w_ref[GLA_W + DIFF_W:, :]))
    y = y * lax.rsqrt(jnp.mean(y * y, axis=-1, keepdims=True) + EPS) * nw_ref[...]
    o_ref[...] = x_ref[...] + y


def _outproj(xf, ya, yb, yc, w, nw):
    n = xf.shape[0]
    tm = ROW_TILE
    row = lambda i: (i, 0)
    const = lambda i: (0, 0)
    return pl.pallas_call(
        _outproj_kernel,
        grid=(n // tm,),
        in_specs=[pl.BlockSpec((tm, D_MODEL), row),
                  pl.BlockSpec((tm, GLA_W), row),
                  pl.BlockSpec((tm, DIFF_W), row),
                  pl.BlockSpec((tm, HGRN_W), row),
                  pl.BlockSpec((MIX_W, D_MODEL), const),
                  pl.BlockSpec((1, D_MODEL), const)],
        out_specs=pl.BlockSpec((tm, D_MODEL), row),
        out_shape=jax.ShapeDtypeStruct((n, D_MODEL), F32),
        compiler_params=pltpu.CompilerParams(dimension_semantics=("parallel",),
                                             vmem_limit_bytes=VMEM_LIMIT),
        name="outproj",
    )(xf, ya, yb, yc, w, nw)


def _block_mean_matrix(heads, dv):
    m = np.kron(np.eye(heads, dtype=np.float32), np.full((dv, dv), 1.0 / dv, np.float32))
    return jnp.asarray(m, BF16)


def kernel(x, norm_pre, norm_post, w_in, w_out, gla_wa2_fwd, gla_ba_fwd, gla_wa2_bwd, gla_ba_bwd, gla_norm,
           diff_lq1, diff_lk1, diff_lq2, diff_lk2, diff_norm, hgrn_lb_logits, hgrn_norm):
    batch, seq_len, d_model = x.shape
    assert d_model == D_MODEL and w_in.shape == (DEPTH, D_MODEL, IN_W)
    assert seq_len % ROW_TILE == 0 and seq_len % Q_TILE == 0 and seq_len % SCAN_TILE == 0
    n = batch * seq_len
    xf = x.reshape(n, D_MODEL)

    a0 = 2 * GLA_QK_W + 2 * GLA_W
    w_perm = jnp.concatenate(
        [w_in[:, :, :a0], w_in[:, :, a0 + 2 * GLA_RANK:], w_in[:, :, a0:a0 + 2 * GLA_RANK],
         jnp.zeros((DEPTH, D_MODEL, A_PAD - 2 * GLA_RANK), w_in.dtype)], axis=-1).astype(BF16)
    wa2 = jnp.zeros((DEPTH, A_PAD, 2 * GLA_QK_W), F32)
    wa2 = wa2.at[:, :GLA_RANK, :GLA_QK_W].set(gla_wa2_fwd)
    wa2 = wa2.at[:, GLA_RANK:2 * GLA_RANK, GLA_QK_W:].set(gla_wa2_bwd).astype(BF16)
    ba = jnp.concatenate([gla_ba_fwd, gla_ba_bwd], axis=-1)
    w_out_b = w_out.astype(BF16)

    inv_freq = ROPE_THETA ** (-jnp.arange(0, DIFF_HD, 2, dtype=F32) / DIFF_HD)
    ang = jnp.arange(seq_len, dtype=jnp.int32).astype(F32)[:, None] * inv_freq[None, :]
    cos_t = jnp.tile(jnp.cos(ang), (1, 2 * LANES // DIFF_HD))
    sin_t = jnp.tile(jnp.concatenate([-jnp.sin(ang), jnp.sin(ang)], axis=-1), (1, LANES // DIFF_HD))

    tri = _tri_matrices(SCAN_TILE)
    bd_gla = _block_mean_matrix(GLA_HEADS, GLA_DV)
    bd_hgrn = _block_mean_matrix(HGRN_HEADS, HGRN_DV)

    for layer in range(DEPTH):
        lambda_init = 0.8 - 0.6 * math.exp(-0.3 * layer)
        (aq, ak, av, alg, dq, dk, dv, hq, hz, hi, g) = _inproj(
            xf, norm_pre[layer][None, :], w_perm[layer], wa2[layer], ba[layer][None, :], cos_t, sin_t, seq_len)
        yb = _attention(dq, dk, dv, g, diff_norm[layer][None, :], diff_lq1[layer][None, :],
                        diff_lk1[layer][None, :], diff_lq2[layer][None, :], diff_lk2[layer][None, :],
                        batch, seq_len, lambda_init)
        ya = _scan_call(
            functools.partial(_gla_kernel, seq_len=seq_len, tile=SCAN_TILE),
            [aq, ak, alg, av], [GLA_QK_W, GLA_QK_W, 2 * GLA_QK_W, GLA_W],
            g, 0, gla_norm[layer][None, :], tri, bd_gla, batch, seq_len, GLA_HEADS, GLA_DK, GLA_DV, "gla_scan")
        yc = _scan_call(
            functools.partial(_hgrn_kernel, seq_len=seq_len, tile=SCAN_TILE, layer=layer),
            [hq, hz, hgrn_lb_logits, hi], [HGRN_QK_W, 2 * HGRN_QK_W, None, HGRN_W],
            g, (GLA_W + DIFF_W) // HGRN_W, hgrn_norm[layer][None, :], tri, bd_hgrn,
            batch, seq_len, HGRN_HEADS, HGRN_DK, HGRN_DV, "hgrn_scan")
        xf = _outproj(xf, ya, yb, yc, w_out_b[layer], norm_post[layer][None, :])
    return xf.reshape(batch, seq_len, D_MODEL)
```

```python
import functools
import math

import numpy as np
import jax
import jax.numpy as jnp
from jax import lax
from jax.experimental import pallas as pl
from jax.experimental.pallas import tpu as pltpu

F32 = jnp.float32
BF16 = jnp.bfloat16

D_MODEL = 1024
DEPTH = 4
GLA_HEADS, GLA_DK, GLA_DV, GLA_RANK = 4, 32, 64, 16
GLA_GATE_TEMP = 16.0
DIFF_HEADS, DIFF_HD = 4, 64
DIFF_DV = 2 * DIFF_HD
HGRN_HEADS, HGRN_DK, HGRN_DV = 4, 64, 64
ROPE_THETA = 10000.0
EPS = 1e-6
F_FLOOR = 1e-30
LOG2E = math.log2(math.e)

GLA_QK_W = GLA_HEADS * GLA_DK
GLA_W = GLA_HEADS * GLA_DV
DIFF_QK_W = DIFF_HEADS * 2 * DIFF_HD
DIFF_W = DIFF_HEADS * DIFF_DV
HGRN_QK_W = HGRN_HEADS * HGRN_DK
HGRN_W = HGRN_HEADS * HGRN_DV
MIX_W = GLA_W + DIFF_W + HGRN_W
IN_W = 2 * GLA_QK_W + 2 * GLA_W + 2 * GLA_RANK + 2 * DIFF_QK_W + 2 * DIFF_W + 3 * HGRN_QK_W + 2 * HGRN_W

LANES = 128
A_PAD = LANES
C_GLA = 0
C_DIFF = C_GLA + 2 * GLA_QK_W + 2 * GLA_W
C_HGRN = C_DIFF + 2 * DIFF_QK_W + 2 * DIFF_W
C_A = C_HGRN + 3 * HGRN_QK_W + 2 * HGRN_W
IN_WP = C_A + A_PAD

VMEM_LIMIT = 56 * 1024 * 1024

ROW_TILE = 256
OUT_ROW_TILE = 512
Q_TILE = 256
SCAN_TILE = 128


def _dot(a, b):
    return jnp.dot(a, b, preferred_element_type=F32)


def _dot_nt(a, b):
    return lax.dot_general(a, b, (((1,), (1,)), ((), ())), preferred_element_type=F32)


def _dot_tn(a, b):
    return lax.dot_general(a, b, (((0,), (0,)), ((), ())), preferred_element_type=F32)


def _split_bf16(x):
    hi = x.astype(BF16)
    lo = (x - hi.astype(F32)).astype(BF16)
    return hi, lo


def _sigmoid(x):
    return 1.0 / (1.0 + jnp.exp(-x))


def _silu(x):
    return x * _sigmoid(x)


def _inproj_kernel(x_ref, nw_ref, w_ref, wa2_ref, ba_ref, cos_ref, sin_ref,
                   aq_ref, ak_ref, av_ref, alg_ref, dq_ref, dk_ref, dv_ref,
                   hq_ref, hz_ref, hi_ref, g_ref):
    x = x_ref[...]
    h = x * lax.rsqrt(jnp.mean(x * x, axis=-1, keepdims=True) + EPS) * nw_ref[...]
    hb = h.astype(BF16)

    def proj(c0, width):
        return _dot(hb, w_ref[:, c0:c0 + width])

    p = proj(C_GLA, 2 * GLA_QK_W + 2 * GLA_W)
    aq_ref[...] = p[:, :GLA_QK_W] * (GLA_DK ** -0.5)
    ak_ref[...] = p[:, GLA_QK_W:2 * GLA_QK_W]
    av_ref[...] = p[:, 2 * GLA_QK_W:2 * GLA_QK_W + GLA_W].astype(BF16)
    g_ref[:, 0:GLA_W] = p[:, 2 * GLA_QK_W + GLA_W:]
    a = proj(C_A, A_PAD)
    zz = _dot(a.astype(BF16), wa2_ref[...]) + ba_ref[...]
    alg_ref[...] = (jnp.minimum(zz, 0.0) - jnp.log1p(jnp.exp(-jnp.abs(zz)))) * (LOG2E / GLA_GATE_TEMP)

    cos = cos_ref[...]
    sin = sin_ref[...]
    lane = lax.broadcasted_iota(jnp.int32, (1, LANES), 1)
    first_half = (lane % DIFF_HD) < (DIFF_HD // 2)

    def rope_store(c0, out_ref, scale):
        pq = proj(c0, DIFF_QK_W)
        for j in range(DIFF_QK_W // LANES):
            xs = pq[:, j * LANES:(j + 1) * LANES]
            partner = jnp.where(first_half,
                                pltpu.roll(xs, LANES - DIFF_HD // 2, 1),
                                pltpu.roll(xs, DIFF_HD // 2, 1))
            r = xs * cos + partner * sin
            if scale != 1.0:
                r = r * scale
            out_ref[:, j * LANES:(j + 1) * LANES] = r.astype(BF16)

    rope_store(C_DIFF, dq_ref, DIFF_HD ** -0.5 * LOG2E)
    rope_store(C_DIFF + DIFF_QK_W, dk_ref, 1.0)
    p = proj(C_DIFF + 2 * DIFF_QK_W, 2 * DIFF_W)
    dv_ref[...] = p[:, :DIFF_W].astype(BF16)
    g_ref[:, GLA_W:GLA_W + DIFF_W] = p[:, DIFF_W:]

    p = proj(C_HGRN, 3 * HGRN_QK_W + 2 * HGRN_W)
    hq_ref[...] = p[:, :HGRN_QK_W]
    hz_ref[...] = p[:, HGRN_QK_W:3 * HGRN_QK_W]
    hi_ref[...] = p[:, 3 * HGRN_QK_W:3 * HGRN_QK_W + HGRN_W].astype(BF16)
    g_ref[:, GLA_W + DIFF_W:] = p[:, 3 * HGRN_QK_W + HGRN_W:]


def _inproj(xf, nw, w, wa2, ba, cos_t, sin_t, seq_len):
    n = xf.shape[0]
    tm = ROW_TILE
    n_pos_tiles = seq_len // tm
    row = lambda i: (i, 0)
    const = lambda i: (0, 0)
    pos = lambda i: (i % n_pos_tiles, 0)
    widths = [(GLA_QK_W, F32), (GLA_QK_W, F32), (GLA_W, BF16), (2 * GLA_QK_W, F32),
              (DIFF_QK_W, BF16), (DIFF_QK_W, BF16), (DIFF_W, BF16),
              (HGRN_QK_W, F32), (2 * HGRN_QK_W, F32), (HGRN_W, BF16), (MIX_W, F32)]
    return pl.pallas_call(
        _inproj_kernel,
        grid=(n // tm,),
        in_specs=[pl.BlockSpec((tm, D_MODEL), row),
                  pl.BlockSpec((1, D_MODEL), const),
                  pl.BlockSpec((D_MODEL, IN_WP), const),
                  pl.BlockSpec((A_PAD, 2 * GLA_QK_W), const),
                  pl.BlockSpec((1, 2 * GLA_QK_W), const),
                  pl.BlockSpec((tm, LANES), pos),
                  pl.BlockSpec((tm, LANES), pos)],
        out_specs=[pl.BlockSpec((tm, wd), row) for wd, _ in widths],
        out_shape=[jax.ShapeDtypeStruct((n, wd), dt) for wd, dt in widths],
        compiler_params=pltpu.CompilerParams(dimension_semantics=("parallel",),
                                             vmem_limit_bytes=VMEM_LIMIT),
        name="inproj",
    )(xf, nw, w, wa2, ba, cos_t, sin_t)


ONES_ROWS = 16
KEY_CHUNK = 512
SCORE_LOOKAHEAD = 6


def _attn_kernel(q_ref, k_ref, v_ref, g_ref, nw_ref, lq1_ref, lk1_ref, lq2_ref, lk2_ref, o_ref, vt_scr,
                 *, lambda_init):
    seq_len = k_ref.shape[0]
    vt_scr[0:DIFF_DV, :] = v_ref[...].astype(F32).T.astype(BF16)
    vt_scr[DIFF_DV:, :] = jnp.ones((ONES_ROWS, seq_len), BF16)

    lam = (jnp.exp(jnp.sum(lq1_ref[...] * lk1_ref[...], axis=-1, keepdims=True))
           - jnp.exp(jnp.sum(lq2_ref[...] * lk2_ref[...], axis=-1, keepdims=True)) + lambda_init)
    lane = lax.broadcasted_iota(jnp.int32, (1, 2 * DIFF_HD), 1)
    nk = seq_len // KEY_CHUNK
    items = [(qi, c, j) for qi in range(seq_len // Q_TILE) for c in range(2) for j in range(nk)]
    qmap_cache = {}

    def qmap(qi, c):
        if (qi, c) not in qmap_cache:
            q = q_ref[qi * Q_TILE:(qi + 1) * Q_TILE, :]
            qmap_cache[(qi, c)] = jnp.where((lane // DIFF_HD) == c, q, jnp.zeros_like(q))
        return qmap_cache[(qi, c)]

    def scores(item):
        qi, c, j = item
        return _dot_nt(k_ref[j * KEY_CHUNK:(j + 1) * KEY_CHUNK, :], qmap(qi, c))

    ahead = [scores(it) for it in items[:SCORE_LOOKAHEAD]]
    parts, maxes, map_out = [], [], []
    for n, (qi, c, j) in enumerate(items):
        st = ahead.pop(0)
        if n + SCORE_LOOKAHEAD < len(items):
            ahead.append(scores(items[n + SCORE_LOOKAHEAD]))
        m = jnp.max(st, axis=0, keepdims=True)
        e = jnp.exp2(st - m).astype(BF16)
        parts.append(_dot(vt_scr[:, j * KEY_CHUNK:(j + 1) * KEY_CHUNK], e))
        maxes.append(m)
        if j < nk - 1:
            continue
        m_all = functools.reduce(jnp.maximum, maxes)
        tot = parts[0] * jnp.exp2(maxes[0] - m_all)
        for part, mj in zip(parts[1:], maxes[1:]):
            tot = tot + part * jnp.exp2(mj - m_all)
        map_out.append(tot[0:DIFF_DV, :] * (1.0 / tot[DIFF_DV:DIFF_DV + 1, :]))
        parts, maxes = [], []
        if c == 0:
            continue
        rows = slice(qi * Q_TILE, (qi + 1) * Q_TILE)
        o = (map_out[0] - lam * map_out[1]).T
        map_out = []
        o = o * lax.rsqrt(jnp.mean(o * o, axis=-1, keepdims=True) + EPS) * nw_ref[...] * (1.0 - lambda_init)
        o_ref[rows, :] = (o * _silu(g_ref[rows, :])).astype(BF16)


def _attention(dq, dk, dv, g, nw, lq1, lk1, lq2, lk2, batch, seq_len, lambda_init):
    n = dq.shape[0]
    bh = lambda b, h: (b, h)
    gmap = lambda b, h: (b, GLA_W // DIFF_DV + h)
    const = lambda b, h: (0, 0)
    small = pl.BlockSpec((1, DIFF_HD), const)
    return pl.pallas_call(
        functools.partial(_attn_kernel, lambda_init=lambda_init),
        grid=(batch, DIFF_HEADS),
        in_specs=[pl.BlockSpec((seq_len, 2 * DIFF_HD), bh),
                  pl.BlockSpec((seq_len, 2 * DIFF_HD), bh),
                  pl.BlockSpec((seq_len, DIFF_DV), bh),
                  pl.BlockSpec((seq_len, DIFF_DV), gmap),
                  pl.BlockSpec((1, DIFF_DV), const),
                  small, small, small, small],
        out_specs=pl.BlockSpec((seq_len, DIFF_DV), bh),
        out_shape=jax.ShapeDtypeStruct((n, DIFF_W), BF16),
        scratch_shapes=[pltpu.VMEM((DIFF_DV + ONES_ROWS, seq_len), BF16)],
        compiler_params=pltpu.CompilerParams(dimension_semantics=("parallel", "parallel"),
                                             vmem_limit_bytes=VMEM_LIMIT),
        name="diff_attn",
    )(dq, dk, dv, g, nw, lq1, lk1, lq2, lk2)


def _scan_levels(tile):
    nlev = int(math.log2(tile))
    assert 1 << nlev == tile
    return [tile >> (j + 1) for j in range(nlev)]


def _tri_matrices(tile):
    idx = np.arange(tile)
    tril = (idx[None, :] <= idx[:, None]).astype(np.float32)
    return jnp.asarray(tril, BF16), jnp.asarray(tril.T, BF16)


def _bidir_scan(load_tile, v_ref, g_ref, nw, tril_ref, triu_ref, bd_ref, y_ref,
                pf_scr, sb_scr, o_scr, qb_scr, kb_scr, totb_scr, stf_scr, stb_scr,
                *, seq_len, tile, heads, dk, dv):
    t_ = tile
    levels = _scan_levels(t_)
    nt = seq_len // t_
    w = heads * dk
    n_vt = heads * dv // LANES
    hpv = LANES // dv
    sub8 = lax.broadcasted_iota(jnp.int32, (8, 1), 0)
    lane = lax.broadcasted_iota(jnp.int32, (1, LANES), 1)
    xr = (lax.broadcasted_iota(jnp.int32, (t_, hpv * t_), 0)
          ^ (lax.broadcasted_iota(jnp.int32, (t_, hpv * t_), 1) & (t_ - 1)))
    vrow = lax.broadcasted_iota(jnp.int32, (LANES, LANES), 0)
    kcol = lax.broadcasted_iota(jnp.int32, (LANES, LANES), 1)

    def tile_info(p):
        first = p * hpv * dk
        cols = slice((first // LANES) * LANES, (first // LANES + 1) * LANES)
        off = first % LANES
        kmasks = [(lane >= off + r * dk) & (lane < off + (r + 1) * dk) for r in range(hpv)]
        vmasks = [(lane >= r * dv) & (lane < (r + 1) * dv) for r in range(hpv)]
        valid = (vrow // dv) == ((kcol - off) // dk)
        valid = valid & (kcol >= off) & (kcol < off + hpv * dk)
        return cols, kmasks, vmasks, valid

    def stack_heads(x, masks):
        return jnp.concatenate([jnp.where(mk, x, jnp.zeros_like(x)) for mk in masks], axis=0)

    stf_scr[...] = jnp.zeros_like(stf_scr)
    stb_scr[...] = jnp.zeros_like(stb_scr)

    def pass_fwd(i, carry):
        r0 = pl.multiple_of(i * t_, t_)
        rows = pl.ds(r0, t_)
        q, kf, kb, lgf, lgb = load_tile(rows)
        hi_f, lo_f = _split_bf16(lgf)
        hi_b, lo_b = _split_bf16(lgb)
        pp = _dot(tril_ref[...], jnp.concatenate([hi_f, lo_f], axis=1))
        ss = _dot(triu_ref[...], jnp.concatenate([hi_b, lo_b], axis=1))
        pf = pp[:, :w] + pp[:, w:]
        sb = ss[:, :w] + ss[:, w:]
        pf_scr[...] = pf
        sb_scr[...] = sb
        row_cache = {}

        def brow(scr, r):
            key = (id(scr), r)
            if key not in row_cache:
                row_cache[key] = jnp.broadcast_to(scr[r:r + 1, :], (8, w))
            return row_cache[key]

        tgt = [[] for _ in levels]
        src = [[] for _ in levels]
        for j in range(t_ // 8):
            r8 = slice(8 * j, 8 * j + 8)
            pf_t = pf[r8]
            sb_t = sb[r8]
            for li, m in enumerate(levels):
                if m >= 8:
                    mid = (8 * j // (2 * m)) * 2 * m + m
                    gf, gb = brow(pf_scr, mid - 1), brow(sb_scr, mid)
                    if (8 * j // m) % 2 == 0:
                        tgt[li].append(sb_t - gb)
                        src[li].append(gf - pf_t)
                    else:
                        tgt[li].append(pf_t - gf)
                        src[li].append(gb - sb_t)
                    continue
                odd = (sub8 & m) != 0
                if m == 1:
                    tgt[li].append(jnp.where(odd, lgf[r8], lgb[r8]))
                    continue
                if m == 4:
                    gf, gb = brow(pf_scr, 8 * j + 3), brow(sb_scr, 8 * j + 4)
                else:
                    low = sub8 < 4
                    gf = jnp.where(low, brow(pf_scr, 8 * j + 1), brow(pf_scr, 8 * j + 5))
                    gb = jnp.where(low, brow(sb_scr, 8 * j + 2), brow(sb_scr, 8 * j + 6))
                d_f = pf_t - gf
                d_b = sb_t - gb
                tgt[li].append(jnp.where(odd, d_f, d_b))
                src[li].append(-jnp.where(odd, d_b, d_f))

        def pow2_16(pieces):
            return jnp.exp2(jnp.concatenate(pieces, axis=0)).astype(BF16)

        q16 = q.astype(BF16)
        kf16 = kf.astype(BF16)
        kb16 = kb.astype(BF16)
        row = lax.broadcasted_iota(jnp.int32, (t_, 1), 0)
        lhs = [q16 * pow2_16(x) for x in tgt]
        rhs = []
        for li, m in enumerate(levels):
            if m >= 16:
                kk = jnp.concatenate([(kb16 if (r // m) % 2 else kf16)[r:r + m] for r in range(0, t_, m)], axis=0)
            else:
                kk = jnp.where((row & m) != 0, kb16, kf16)
            rhs.append(kk if m == 1 else kk * pow2_16(src[li]))
        ksum16 = (kf + kb).astype(BF16)
        totf = pf_scr[t_ - 1:t_, :]
        totb = sb_scr[0:1, :]
        qf = q16 * jnp.exp2(pf).astype(BF16)
        kfd = kf16 * jnp.exp2(totf - pf).astype(BF16)
        qb_scr[rows, :] = q16 * jnp.exp2(sb).astype(BF16)
        kb_scr[rows, :] = kb16 * jnp.exp2(totb - sb).astype(BF16)
        dec_f = jnp.exp2(totf)
        totb_scr[i] = jnp.broadcast_to(jnp.exp2(totb), (8, w))

        v = v_ref[rows, :]
        outs = []
        for p in range(n_vt):
            cols, kmasks, vmasks, valid = tile_info(p)
            a = _dot_nt(lhs[0][:, cols], stack_heads(rhs[0][:, cols], kmasks))
            for li in range(1, len(levels)):
                a = jnp.where(xr < 2 * levels[li],
                              _dot_nt(lhs[li][:, cols], stack_heads(rhs[li][:, cols], kmasks)), a)
            a = jnp.where(xr == 0, _dot_nt(q16[:, cols], stack_heads(ksum16[:, cols], kmasks)), a)
            v_p = v[:, p * LANES:(p + 1) * LANES]
            st = stf_scr[p]
            outs.append(_dot(a.astype(BF16), stack_heads(v_p, vmasks)) + _dot_nt(qf[:, cols], st.astype(BF16)))
            stf_scr[p] = jnp.where(valid, st * dec_f[:, cols] + _dot_tn(v_p, kfd[:, cols]), 0.0)
        o_scr[rows, :] = jnp.concatenate(outs, axis=1)
        return carry

    lax.fori_loop(0, nt, pass_fwd, 0, unroll=2)

    def pass_bwd(n, carry):
        i = nt - 1 - n
        r0 = pl.multiple_of(i * t_, t_)
        rows = pl.ds(r0, t_)
        v = v_ref[rows, :]
        qb = qb_scr[rows, :]
        kbd = kb_scr[rows, :]
        dec_b = totb_scr[i][0:1, :]
        outs = []
        for p in range(n_vt):
            cols, _, _, valid = tile_info(p)
            st = stb_scr[p]
            outs.append(_dot_nt(qb[:, cols], st.astype(BF16)))
            stb_scr[p] = jnp.where(valid, st * dec_b[:, cols] + _dot_tn(v[:, p * LANES:(p + 1) * LANES],
                                                                       kbd[:, cols]), 0.0)
        o = o_scr[rows, :] + jnp.concatenate(outs, axis=1)
        hi, lo = _split_bf16(o * o)
        bd = bd_ref[...]
        ms = _dot(hi, bd) + _dot(lo, bd)
        y = o * lax.rsqrt(ms + EPS) * nw * _silu(g_ref[rows, :])
        y_ref[rows, :] = y.astype(BF16)
        return carry

    lax.fori_loop(0, nt, pass_bwd, 0, unroll=4)


def _gla_kernel(q_ref, k_ref, lg_ref, v_ref, g_ref, nw_ref, tril_ref, triu_ref, bd_ref, y_ref, *scratch,
                seq_len, tile):
    def load_tile(rows):
        k = k_ref[rows, :]
        lg = lg_ref[rows, :]
        return q_ref[rows, :], k, k, lg[:, :GLA_QK_W], lg[:, GLA_QK_W:]

    nw = jnp.concatenate([nw_ref[...]] * GLA_HEADS, axis=1)
    _bidir_scan(load_tile, v_ref, g_ref, nw, tril_ref, triu_ref, bd_ref, y_ref, *scratch,
                seq_len=seq_len, tile=tile, heads=GLA_HEADS, dk=GLA_DK, dv=GLA_DV)


def _hgrn_kernel(q_ref, z_ref, lbl_ref, v_ref, g_ref, nw_ref, tril_ref, triu_ref, bd_ref, y_ref, *scratch,
                 seq_len, tile, layer):
    logits = lbl_ref[...]
    e = jnp.exp(logits - jnp.max(logits, axis=0, keepdims=True))
    p = e / jnp.sum(e, axis=0, keepdims=True)
    lb = jnp.zeros((1, HGRN_QK_W), F32)
    for j in range(1, layer + 1):
        lb = lb + p[j:j + 1, :]

    def gates(z):
        t = jnp.exp(-jnp.abs(z))
        r = 1.0 / (1.0 + t)
        pos = z >= 0.0
        sig = jnp.where(pos, r, t * r)
        sig_neg = jnp.where(pos, t * r, r)
        f = lb + (1.0 - lb) * sig
        return jnp.log(jnp.maximum(f, F_FLOOR)) * LOG2E, (1.0 - lb) * sig_neg

    def load_tile(rows):
        z = z_ref[rows, :]
        lgf, kf = gates(z[:, :HGRN_QK_W])
        lgb, kb = gates(z[:, HGRN_QK_W:])
        return q_ref[rows, :], kf, kb, lgf, lgb

    nw = jnp.concatenate([nw_ref[...]] * HGRN_HEADS, axis=1)
    _bidir_scan(load_tile, v_ref, g_ref, nw, tril_ref, triu_ref, bd_ref, y_ref, *scratch,
                seq_len=seq_len, tile=tile, heads=HGRN_HEADS, dk=HGRN_DK, dv=HGRN_DV)


def _scan_scratch(seq_len, tile, heads, dk, dv):
    w = heads * dk
    n_vt = heads * dv // LANES
    return [pltpu.VMEM((tile, w), F32),
            pltpu.VMEM((tile, w), F32),
            pltpu.VMEM((seq_len, heads * dv), F32),
            pltpu.VMEM((seq_len, w), BF16),
            pltpu.VMEM((seq_len, w), BF16),
            pltpu.VMEM((seq_len // tile, 8, w), F32),
            pltpu.VMEM((n_vt, LANES, LANES), F32),
            pltpu.VMEM((n_vt, LANES, LANES), F32)]


def _scan_call(kernel_fn, inputs, in_widths, g, g_block, nw, tri, bd, batch, seq_len, heads, dk, dv, name):
    n = batch * seq_len
    bmap = lambda b: (b, 0)
    const = lambda b: (0, 0)
    in_specs = [pl.BlockSpec((seq_len, wd), bmap) if wd is not None else pl.BlockSpec(arr.shape, const)
                for arr, wd in zip(inputs, in_widths)]
    in_specs += [pl.BlockSpec((seq_len, heads * dv), lambda b: (b, g_block)),
                 pl.BlockSpec(nw.shape, const),
                 pl.BlockSpec(tri[0].shape, const),
                 pl.BlockSpec(tri[1].shape, const),
                 pl.BlockSpec(bd.shape, const)]
    return pl.pallas_call(
        kernel_fn,
        grid=(batch,),
        in_specs=in_specs,
        out_specs=pl.BlockSpec((seq_len, heads * dv), bmap),
        out_shape=jax.ShapeDtypeStruct((n, heads * dv), BF16),
        scratch_shapes=_scan_scratch(seq_len, SCAN_TILE, heads, dk, dv),
        compiler_params=pltpu.CompilerParams(dimension_semantics=("parallel",),
                                             vmem_limit_bytes=VMEM_LIMIT),
        name=name,
    )(*inputs, g, nw, tri[0], tri[1], bd)


def _outproj_kernel(x_ref, ya_ref, yb_ref, yc_ref, w_ref, nw_ref, o_ref):
    y = (_dot(ya_ref[...], w_ref[0:GLA_W, :])
         + _dot(yb_ref[...], w_ref[GLA_W:GLA_W + DIFF_W, :])
         + _dot(yc_ref[...], w_ref[GLA_W + DIFF_W:, :]))
    y = y * lax.rsqrt(jnp.mean(y * y, axis=-1, keepdims=True) + EPS) * nw_ref[...]
    o_ref[...] = x_ref[...] + y


def _outproj(xf, ya, yb, yc, w, nw):
    n = xf.shape[0]
    tm = OUT_ROW_TILE
    row = lambda i: (i, 0)
    const = lambda i: (0, 0)
    return pl.pallas_call(
        _outproj_kernel,
        grid=(n // tm,),
        in_specs=[pl.BlockSpec((tm, D_MODEL), row),
                  pl.BlockSpec((tm, GLA_W), row),
                  pl.BlockSpec((tm, DIFF_W), row),
                  pl.BlockSpec((tm, HGRN_W), row),
                  pl.BlockSpec((MIX_W, D_MODEL), const),
                  pl.BlockSpec((1, D_MODEL), const)],
        out_specs=pl.BlockSpec((tm, D_MODEL), row),
        out_shape=jax.ShapeDtypeStruct((n, D_MODEL), F32),
        compiler_params=pltpu.CompilerParams(dimension_semantics=("parallel",),
                                             vmem_limit_bytes=VMEM_LIMIT),
        name="outproj",
    )(xf, ya, yb, yc, w, nw)


def _block_mean_matrix(heads, dv):
    m = np.kron(np.eye(heads, dtype=np.float32), np.full((dv, dv), 1.0 / dv, np.float32))
    return jnp.asarray(m, BF16)


def kernel(x, norm_pre, norm_post, w_in, w_out, gla_wa2_fwd, gla_ba_fwd, gla_wa2_bwd, gla_ba_bwd, gla_norm,
           diff_lq1, diff_lk1, diff_lq2, diff_lk2, diff_norm, hgrn_lb_logits, hgrn_norm):
    batch, seq_len, d_model = x.shape
    assert d_model == D_MODEL and w_in.shape == (DEPTH, D_MODEL, IN_W)
    assert seq_len % ROW_TILE == 0 and seq_len % Q_TILE == 0 and seq_len % SCAN_TILE == 0
    assert seq_len % KEY_CHUNK == 0 and (batch * seq_len) % OUT_ROW_TILE == 0
    n = batch * seq_len
    xf = x.reshape(n, D_MODEL)

    a0 = 2 * GLA_QK_W + 2 * GLA_W
    w_perm = jnp.concatenate(
        [w_in[:, :, :a0], w_in[:, :, a0 + 2 * GLA_RANK:], w_in[:, :, a0:a0 + 2 * GLA_RANK],
         jnp.zeros((DEPTH, D_MODEL, A_PAD - 2 * GLA_RANK), w_in.dtype)], axis=-1).astype(BF16)
    wa2 = jnp.zeros((DEPTH, A_PAD, 2 * GLA_QK_W), F32)
    wa2 = wa2.at[:, :GLA_RANK, :GLA_QK_W].set(gla_wa2_fwd)
    wa2 = wa2.at[:, GLA_RANK:2 * GLA_RANK, GLA_QK_W:].set(gla_wa2_bwd).astype(BF16)
    ba = jnp.concatenate([gla_ba_fwd, gla_ba_bwd], axis=-1)
    w_out_b = w_out.astype(BF16)

    inv_freq = ROPE_THETA ** (-jnp.arange(0, DIFF_HD, 2, dtype=F32) / DIFF_HD)
    ang = jnp.arange(seq_len, dtype=jnp.int32).astype(F32)[:, None] * inv_freq[None, :]
    cos_t = jnp.tile(jnp.cos(ang), (1, 2 * LANES // DIFF_HD))
    sin_t = jnp.tile(jnp.concatenate([-jnp.sin(ang), jnp.sin(ang)], axis=-1), (1, LANES // DIFF_HD))

    tri = _tri_matrices(SCAN_TILE)
    bd_gla = _block_mean_matrix(GLA_HEADS, GLA_DV)
    bd_hgrn = _block_mean_matrix(HGRN_HEADS, HGRN_DV)

    for layer in range(DEPTH):
        lambda_init = 0.8 - 0.6 * math.exp(-0.3 * layer)
        (aq, ak, av, alg, dq, dk, dv, hq, hz, hi, g) = _inproj(
            xf, norm_pre[layer][None, :], w_perm[layer], wa2[layer], ba[layer][None, :], cos_t, sin_t, seq_len)
        yb = _attention(dq, dk, dv, g, diff_norm[layer][None, :], diff_lq1[layer][None, :],
                        diff_lk1[layer][None, :], diff_lq2[layer][None, :], diff_lk2[layer][None, :],
                        batch, seq_len, lambda_init)
        ya = _scan_call(
            functools.partial(_gla_kernel, seq_len=seq_len, tile=SCAN_TILE),
            [aq, ak, alg, av], [GLA_QK_W, GLA_QK_W, 2 * GLA_QK_W, GLA_W],
            g, 0, gla_norm[layer][None, :], tri, bd_gla, batch, seq_len, GLA_HEADS, GLA_DK, GLA_DV, "gla_scan")
        yc = _scan_call(
            functools.partial(_hgrn_kernel, seq_len=seq_len, tile=SCAN_TILE, layer=layer),
            [hq, hz, hgrn_lb_logits, hi], [HGRN_QK_W, 2 * HGRN_QK_W, None, HGRN_W],
            g, (GLA_W + DIFF_W) // HGRN_W, hgrn_norm[layer][None, :], tri, bd_hgrn,
            batch, seq_len, HGRN_HEADS, HGRN_DK, HGRN_DV, "hgrn_scan")
        xf = _outproj(xf, ya, yb, yc, w_out_b[layer], norm_post[layer][None, :])
    return xf.reshape(batch, seq_len, D_MODEL)
```

```python
import functools
import math

import numpy as np
import jax
import jax.numpy as jnp
from jax import lax
from jax.experimental import pallas as pl
from jax.experimental.pallas import tpu as pltpu

F32 = jnp.float32
BF16 = jnp.bfloat16

D_MODEL = 1024
DEPTH = 4
GLA_HEADS, GLA_DK, GLA_DV, GLA_RANK = 4, 32, 64, 16
GLA_GATE_TEMP = 16.0
DIFF_HEADS, DIFF_HD = 4, 64
DIFF_DV = 2 * DIFF_HD
HGRN_HEADS, HGRN_DK, HGRN_DV = 4, 64, 64
ROPE_THETA = 10000.0
EPS = 1e-6
F_FLOOR = 1e-30
LOG2E = math.log2(math.e)

GLA_QK_W = GLA_HEADS * GLA_DK
GLA_W = GLA_HEADS * GLA_DV
DIFF_QK_W = DIFF_HEADS * 2 * DIFF_HD
DIFF_W = DIFF_HEADS * DIFF_DV
HGRN_QK_W = HGRN_HEADS * HGRN_DK
HGRN_W = HGRN_HEADS * HGRN_DV
MIX_W = GLA_W + DIFF_W + HGRN_W
IN_W = 2 * GLA_QK_W + 2 * GLA_W + 2 * GLA_RANK + 2 * DIFF_QK_W + 2 * DIFF_W + 3 * HGRN_QK_W + 2 * HGRN_W

LANES = 128
A_PAD = LANES
C_GLA = 0
C_DIFF = C_GLA + 2 * GLA_QK_W + 2 * GLA_W
C_HGRN = C_DIFF + 2 * DIFF_QK_W + 2 * DIFF_W
C_A = C_HGRN + 3 * HGRN_QK_W + 2 * HGRN_W
IN_WP = C_A + A_PAD

VMEM_LIMIT = 56 * 1024 * 1024

ROW_TILE = 512
OUT_ROW_TILE = 512
Q_TILE = 256
SCAN_TILE = 128


def _dot(a, b):
    return jnp.dot(a, b, preferred_element_type=F32)


def _dot_nt(a, b):
    return lax.dot_general(a, b, (((1,), (1,)), ((), ())), preferred_element_type=F32)


def _dot_tn(a, b):
    return lax.dot_general(a, b, (((0,), (0,)), ((), ())), preferred_element_type=F32)


def _split_bf16(x):
    hi = x.astype(BF16)
    lo = (x - hi.astype(F32)).astype(BF16)
    return hi, lo


def _sigmoid(x):
    return 1.0 / (1.0 + jnp.exp(-x))


def _silu(x):
    return x * _sigmoid(x)


def _inproj_kernel(x_ref, nw_ref, w_ref, wa2_ref, ba_ref, cos_ref, sin_ref,
                   aq_ref, ak_ref, av_ref, alg_ref, dq_ref, dk_ref, dv_ref,
                   hq_ref, hz_ref, hi_ref, g_ref):
    x = x_ref[...]
    h = x * lax.rsqrt(jnp.mean(x * x, axis=-1, keepdims=True) + EPS) * nw_ref[...]
    hb = h.astype(BF16)

    def proj(c0, width):
        return _dot(hb, w_ref[:, c0:c0 + width])

    p = proj(C_GLA, 2 * GLA_QK_W + 2 * GLA_W)
    aq_ref[...] = (p[:, :GLA_QK_W] * (GLA_DK ** -0.5)).astype(BF16)
    ak_ref[...] = p[:, GLA_QK_W:2 * GLA_QK_W].astype(BF16)
    av_ref[...] = p[:, 2 * GLA_QK_W:2 * GLA_QK_W + GLA_W].astype(BF16)
    g_ref[:, 0:GLA_W] = p[:, 2 * GLA_QK_W + GLA_W:]
    a = proj(C_A, A_PAD)
    zz = _dot(a.astype(BF16), wa2_ref[...]) + ba_ref[...]
    alg_ref[...] = (jnp.minimum(zz, 0.0) - jnp.log1p(jnp.exp(-jnp.abs(zz)))) * (LOG2E / GLA_GATE_TEMP)

    cos = cos_ref[...]
    sin = sin_ref[...]
    lane = lax.broadcasted_iota(jnp.int32, (1, LANES), 1)
    first_half = (lane % DIFF_HD) < (DIFF_HD // 2)

    def rope_store(c0, out_ref, scale):
        pq = proj(c0, DIFF_QK_W)
        for j in range(DIFF_QK_W // LANES):
            xs = pq[:, j * LANES:(j + 1) * LANES]
            partner = jnp.where(first_half,
                                pltpu.roll(xs, LANES - DIFF_HD // 2, 1),
                                pltpu.roll(xs, DIFF_HD // 2, 1))
            r = xs * cos + partner * sin
            if scale != 1.0:
                r = r * scale
            out_ref[:, j * LANES:(j + 1) * LANES] = r.astype(BF16)

    rope_store(C_DIFF, dq_ref, DIFF_HD ** -0.5 * LOG2E)
    rope_store(C_DIFF + DIFF_QK_W, dk_ref, 1.0)
    p = proj(C_DIFF + 2 * DIFF_QK_W, 2 * DIFF_W)
    dv_ref[...] = p[:, :DIFF_W].astype(BF16)
    g_ref[:, GLA_W:GLA_W + DIFF_W] = p[:, DIFF_W:]

    p = proj(C_HGRN, 3 * HGRN_QK_W + 2 * HGRN_W)
    hq_ref[...] = p[:, :HGRN_QK_W].astype(BF16)
    hz_ref[...] = p[:, HGRN_QK_W:3 * HGRN_QK_W]
    hi_ref[...] = p[:, 3 * HGRN_QK_W:3 * HGRN_QK_W + HGRN_W].astype(BF16)
    g_ref[:, GLA_W + DIFF_W:] = p[:, 3 * HGRN_QK_W + HGRN_W:]


def _inproj(xf, nw, w, wa2, ba, cos_t, sin_t, seq_len):
    n = xf.shape[0]
    tm = ROW_TILE
    n_pos_tiles = seq_len // tm
    row = lambda i: (i, 0)
    const = lambda i: (0, 0)
    pos = lambda i: (i % n_pos_tiles, 0)
    widths = [(GLA_QK_W, BF16), (GLA_QK_W, BF16), (GLA_W, BF16), (2 * GLA_QK_W, F32),
              (DIFF_QK_W, BF16), (DIFF_QK_W, BF16), (DIFF_W, BF16),
              (HGRN_QK_W, BF16), (2 * HGRN_QK_W, F32), (HGRN_W, BF16), (MIX_W, F32)]
    return pl.pallas_call(
        _inproj_kernel,
        grid=(n // tm,),
        in_specs=[pl.BlockSpec((tm, D_MODEL), row),
                  pl.BlockSpec((1, D_MODEL), const),
                  pl.BlockSpec((D_MODEL, IN_WP), const, pipeline_mode=pl.Buffered(1)),
                  pl.BlockSpec((A_PAD, 2 * GLA_QK_W), const),
                  pl.BlockSpec((1, 2 * GLA_QK_W), const),
                  pl.BlockSpec((tm, LANES), pos),
                  pl.BlockSpec((tm, LANES), pos)],
        out_specs=[pl.BlockSpec((tm, wd), row) for wd, _ in widths],
        out_shape=[jax.ShapeDtypeStruct((n, wd), dt) for wd, dt in widths],
        compiler_params=pltpu.CompilerParams(dimension_semantics=("parallel",),
                                             vmem_limit_bytes=VMEM_LIMIT),
        name="inproj",
    )(xf, nw, w, wa2, ba, cos_t, sin_t)


ONES_ROWS = 16
KEY_CHUNK = 512
SCORE_LOOKAHEAD = 6


def _attn_kernel(q_ref, k_ref, v_ref, g_ref, nw_ref, lq1_ref, lk1_ref, lq2_ref, lk2_ref, o_ref, vt_scr,
                 *, lambda_init):
    seq_len = k_ref.shape[0]
    vt_scr[0:DIFF_DV, :] = v_ref[...].astype(F32).T.astype(BF16)
    vt_scr[DIFF_DV:, :] = jnp.ones((ONES_ROWS, seq_len), BF16)

    lam = (jnp.exp(jnp.sum(lq1_ref[...] * lk1_ref[...], axis=-1, keepdims=True))
           - jnp.exp(jnp.sum(lq2_ref[...] * lk2_ref[...], axis=-1, keepdims=True)) + lambda_init)
    lane = lax.broadcasted_iota(jnp.int32, (1, 2 * DIFF_HD), 1)
    nk = seq_len // KEY_CHUNK
    items = [(qi, c, j) for qi in range(seq_len // Q_TILE) for c in range(2) for j in range(nk)]
    qmap_cache = {}

    def qmap(qi, c):
        if (qi, c) not in qmap_cache:
            q = q_ref[qi * Q_TILE:(qi + 1) * Q_TILE, :]
            qmap_cache[(qi, c)] = jnp.where((lane // DIFF_HD) == c, q, jnp.zeros_like(q))
        return qmap_cache[(qi, c)]

    def scores(item):
        qi, c, j = item
        return _dot_nt(k_ref[j * KEY_CHUNK:(j + 1) * KEY_CHUNK, :], qmap(qi, c))

    ahead = [scores(it) for it in items[:SCORE_LOOKAHEAD]]
    parts, maxes, map_out = [], [], []
    for n, (qi, c, j) in enumerate(items):
        st = ahead.pop(0)
        if n + SCORE_LOOKAHEAD < len(items):
            ahead.append(scores(items[n + SCORE_LOOKAHEAD]))
        m = jnp.max(st, axis=0, keepdims=True)
        e = jnp.exp2(st - m).astype(BF16)
        parts.append(_dot(vt_scr[:, j * KEY_CHUNK:(j + 1) * KEY_CHUNK], e))
        maxes.append(m)
        if j < nk - 1:
            continue
        m_all = functools.reduce(jnp.maximum, maxes)
        tot = parts[0] * jnp.exp2(maxes[0] - m_all)
        for part, mj in zip(parts[1:], maxes[1:]):
            tot = tot + part * jnp.exp2(mj - m_all)
        map_out.append(tot[0:DIFF_DV, :] * (1.0 / tot[DIFF_DV:DIFF_DV + 1, :]))
        parts, maxes = [], []
        if c == 0:
            continue
        rows = slice(qi * Q_TILE, (qi + 1) * Q_TILE)
        o = (map_out[0] - lam * map_out[1]).T
        map_out = []
        o = o * lax.rsqrt(jnp.mean(o * o, axis=-1, keepdims=True) + EPS) * nw_ref[...] * (1.0 - lambda_init)
        o_ref[rows, :] = (o * _silu(g_ref[rows, :])).astype(BF16)


def _attention(dq, dk, dv, g, nw, lq1, lk1, lq2, lk2, batch, seq_len, lambda_init):
    n = dq.shape[0]
    bh = lambda b, h: (b, h)
    gmap = lambda b, h: (b, GLA_W // DIFF_DV + h)
    const = lambda b, h: (0, 0)
    small = pl.BlockSpec((1, DIFF_HD), const)
    return pl.pallas_call(
        functools.partial(_attn_kernel, lambda_init=lambda_init),
        grid=(batch, DIFF_HEADS),
        in_specs=[pl.BlockSpec((seq_len, 2 * DIFF_HD), bh),
                  pl.BlockSpec((seq_len, 2 * DIFF_HD), bh),
                  pl.BlockSpec((seq_len, DIFF_DV), bh),
                  pl.BlockSpec((seq_len, DIFF_DV), gmap),
                  pl.BlockSpec((1, DIFF_DV), const),
                  small, small, small, small],
        out_specs=pl.BlockSpec((seq_len, DIFF_DV), bh),
        out_shape=jax.ShapeDtypeStruct((n, DIFF_W), BF16),
        scratch_shapes=[pltpu.VMEM((DIFF_DV + ONES_ROWS, seq_len), BF16)],
        compiler_params=pltpu.CompilerParams(dimension_semantics=("parallel", "parallel"),
                                             vmem_limit_bytes=VMEM_LIMIT),
        name="diff_attn",
    )(dq, dk, dv, g, nw, lq1, lk1, lq2, lk2)


def _scan_levels(tile):
    nlev = int(math.log2(tile))
    assert 1 << nlev == tile
    return [tile >> (j + 1) for j in range(nlev)]


def _tri_matrices(tile):
    idx = np.arange(tile)
    tril = (idx[None, :] <= idx[:, None]).astype(np.float32)
    return jnp.asarray(tril, BF16), jnp.asarray(tril.T, BF16)


def _interleave(generators):
    live = list(generators)
    while live:
        for gen in list(live):
            try:
                next(gen)
            except StopIteration:
                live.remove(gen)


def _run_scans(scans, n_tiles):
    for init, _, _ in scans:
        init()

    def fwd(i, carry):
        _interleave([steps(i) for _, steps, _ in scans])
        return carry

    lax.fori_loop(0, n_tiles, fwd, 0, unroll=2)

    def bwd(n, carry):
        _interleave([steps(n_tiles - 1 - n) for _, _, steps in scans])
        return carry

    lax.fori_loop(0, n_tiles, bwd, 0, unroll=4)


def _make_scan(load_tile, v_ref, g_ref, nw, tril_ref, triu_ref, bd_ref, y_ref,
               pf_scr, sb_scr, o_scr, qb_scr, kb_scr, totb_scr, stf_scr, stb_scr,
               *, tile, heads, dk, dv):
    t_ = tile
    levels = _scan_levels(t_)
    w = heads * dk
    n_vt = heads * dv // LANES
    hpv = LANES // dv
    sub8 = lax.broadcasted_iota(jnp.int32, (8, 1), 0)
    lane = lax.broadcasted_iota(jnp.int32, (1, LANES), 1)
    xr = (lax.broadcasted_iota(jnp.int32, (t_, hpv * t_), 0)
          ^ (lax.broadcasted_iota(jnp.int32, (t_, hpv * t_), 1) & (t_ - 1)))
    vrow = lax.broadcasted_iota(jnp.int32, (LANES, LANES), 0)
    kcol = lax.broadcasted_iota(jnp.int32, (LANES, LANES), 1)

    def tile_info(p):
        first = p * hpv * dk
        cols = slice((first // LANES) * LANES, (first // LANES + 1) * LANES)
        off = first % LANES
        kmasks = [(lane >= off + r * dk) & (lane < off + (r + 1) * dk) for r in range(hpv)]
        vmasks = [(lane >= r * dv) & (lane < (r + 1) * dv) for r in range(hpv)]
        valid = (vrow // dv) == ((kcol - off) // dk)
        valid = valid & (kcol >= off) & (kcol < off + hpv * dk)
        return cols, kmasks, vmasks, valid

    def stack_heads(x, masks):
        return jnp.concatenate([jnp.where(mk, x, jnp.zeros_like(x)) for mk in masks], axis=0)

    def init():
        stf_scr[...] = jnp.zeros_like(stf_scr)
        stb_scr[...] = jnp.zeros_like(stb_scr)

    def fwd_steps(i):
        r0 = pl.multiple_of(i * t_, t_)
        rows = pl.ds(r0, t_)
        q, kf, kb, lgf, lgb = load_tile(rows)
        hi_f, lo_f = _split_bf16(lgf)
        hi_b, lo_b = _split_bf16(lgb)
        pp = _dot(tril_ref[...], jnp.concatenate([hi_f, lo_f], axis=1))
        ss = _dot(triu_ref[...], jnp.concatenate([hi_b, lo_b], axis=1))
        pf = pp[:, :w] + pp[:, w:]
        sb = ss[:, :w] + ss[:, w:]
        pf_scr[...] = pf
        sb_scr[...] = sb
        yield
        row_cache = {}

        def brow(scr, r):
            key = (id(scr), r)
            if key not in row_cache:
                row_cache[key] = jnp.broadcast_to(scr[r:r + 1, :], (8, w))
            return row_cache[key]

        tgt = [[] for _ in levels]
        src = [[] for _ in levels]
        for j in range(t_ // 8):
            r8 = slice(8 * j, 8 * j + 8)
            pf_t = pf[r8]
            sb_t = sb[r8]
            for li, m in enumerate(levels):
                if m >= 8:
                    mid = (8 * j // (2 * m)) * 2 * m + m
                    gf, gb = brow(pf_scr, mid - 1), brow(sb_scr, mid)
                    if (8 * j // m) % 2 == 0:
                        tgt[li].append(sb_t - gb)
                        src[li].append(gf - pf_t)
                    else:
                        tgt[li].append(pf_t - gf)
                        src[li].append(gb - sb_t)
                    continue
                odd = (sub8 & m) != 0
                if m == 1:
                    tgt[li].append(jnp.where(odd, lgf[r8], lgb[r8]))
                    continue
                if m == 4:
                    gf, gb = brow(pf_scr, 8 * j + 3), brow(sb_scr, 8 * j + 4)
                else:
                    low = sub8 < 4
                    gf = jnp.where(low, brow(pf_scr, 8 * j + 1), brow(pf_scr, 8 * j + 5))
                    gb = jnp.where(low, brow(sb_scr, 8 * j + 2), brow(sb_scr, 8 * j + 6))
                d_f = pf_t - gf
                d_b = sb_t - gb
                tgt[li].append(jnp.where(odd, d_f, d_b))
                src[li].append(-jnp.where(odd, d_b, d_f))
            if j % 4 == 3:
                yield

        def pow2_16(pieces):
            return jnp.exp2(jnp.concatenate(pieces, axis=0)).astype(BF16)

        q16 = q.astype(BF16)
        kf16 = kf.astype(BF16)
        kb16 = kb.astype(BF16)
        row = lax.broadcasted_iota(jnp.int32, (t_, 1), 0)
        lhs, rhs = [], []
        for li, m in enumerate(levels):
            lhs.append(q16 * pow2_16(tgt[li]))
            if m >= 16:
                kk = jnp.concatenate([(kb16 if (r // m) % 2 else kf16)[r:r + m] for r in range(0, t_, m)], axis=0)
            else:
                kk = jnp.where((row & m) != 0, kb16, kf16)
            rhs.append(kk if m == 1 else kk * pow2_16(src[li]))
            yield
        ksum16 = (kf + kb).astype(BF16)
        totf = pf_scr[t_ - 1:t_, :]
        totb = sb_scr[0:1, :]
        qf = q16 * jnp.exp2(pf).astype(BF16)
        kfd = kf16 * jnp.exp2(totf - pf).astype(BF16)
        yield
        qb_scr[rows, :] = q16 * jnp.exp2(sb).astype(BF16)
        kb_scr[rows, :] = kb16 * jnp.exp2(totb - sb).astype(BF16)
        dec_f = jnp.exp2(totf)
        totb_scr[i] = jnp.broadcast_to(jnp.exp2(totb), (8, w))
        yield

        v = v_ref[rows, :]
        outs = []
        for p in range(n_vt):
            cols, kmasks, vmasks, valid = tile_info(p)
            a = _dot_nt(lhs[0][:, cols], stack_heads(rhs[0][:, cols], kmasks))
            yield
            for li in range(1, len(levels)):
                a = jnp.where(xr < 2 * levels[li],
                              _dot_nt(lhs[li][:, cols], stack_heads(rhs[li][:, cols], kmasks)), a)
                yield
            a = jnp.where(xr == 0, _dot_nt(q16[:, cols], stack_heads(ksum16[:, cols], kmasks)), a)
            yield
            v_p = v[:, p * LANES:(p + 1) * LANES]
            st = stf_scr[p]
            outs.append(_dot(a.astype(BF16), stack_heads(v_p, vmasks)) + _dot_nt(qf[:, cols], st.astype(BF16)))
            stf_scr[p] = jnp.where(valid, st * dec_f[:, cols] + _dot_tn(v_p, kfd[:, cols]), 0.0)
            yield
        o_scr[rows, :] = jnp.concatenate(outs, axis=1)

    def bwd_steps(i):
        r0 = pl.multiple_of(i * t_, t_)
        rows = pl.ds(r0, t_)
        v = v_ref[rows, :]
        qb = qb_scr[rows, :]
        kbd = kb_scr[rows, :]
        dec_b = totb_scr[i][0:1, :]
        outs = []
        for p in range(n_vt):
            cols, _, _, valid = tile_info(p)
            st = stb_scr[p]
            outs.append(_dot_nt(qb[:, cols], st.astype(BF16)))
            stb_scr[p] = jnp.where(valid, st * dec_b[:, cols] + _dot_tn(v[:, p * LANES:(p + 1) * LANES],
                                                                       kbd[:, cols]), 0.0)
            yield
        o = o_scr[rows, :] + jnp.concatenate(outs, axis=1)
        hi, lo = _split_bf16(o * o)
        bd = bd_ref[...]
        ms = _dot(hi, bd) + _dot(lo, bd)
        yield
        y = o * lax.rsqrt(ms + EPS) * nw * _silu(g_ref[rows, :])
        y_ref[rows, :] = y.astype(BF16)

    return init, fwd_steps, bwd_steps


N_SCAN_SCRATCH = 8


def _scans_kernel(aq_ref, ak_ref, alg_ref, av_ref, ag_ref, anw_ref, abd_ref,
                  hq_ref, hz_ref, lbl_ref, hv_ref, hg_ref, hnw_ref, hbd_ref, tril_ref, triu_ref,
                  ya_ref, yc_ref, *scratch, seq_len, tile, layer):
    def gla_tile(rows):
        k = ak_ref[rows, :]
        lg = alg_ref[rows, :]
        return aq_ref[rows, :], k, k, lg[:, :GLA_QK_W], lg[:, GLA_QK_W:]

    gla = _make_scan(gla_tile, av_ref, ag_ref, jnp.concatenate([anw_ref[...]] * GLA_HEADS, axis=1),
                     tril_ref, triu_ref, abd_ref, ya_ref, *scratch[:N_SCAN_SCRATCH],
                     tile=tile, heads=GLA_HEADS, dk=GLA_DK, dv=GLA_DV)

    logits = lbl_ref[...]
    e = jnp.exp(logits - jnp.max(logits, axis=0, keepdims=True))
    share = e / jnp.sum(e, axis=0, keepdims=True)
    lb = jnp.zeros((1, HGRN_QK_W), F32)
    for j in range(1, layer + 1):
        lb = lb + share[j:j + 1, :]

    def gates(z):
        t = jnp.exp(-jnp.abs(z))
        r = 1.0 / (1.0 + t)
        pos = z >= 0.0
        sig = jnp.where(pos, r, t * r)
        sig_neg = jnp.where(pos, t * r, r)
        f = lb + (1.0 - lb) * sig
        return jnp.log(jnp.maximum(f, F_FLOOR)) * LOG2E, (1.0 - lb) * sig_neg

    def hgrn_tile(rows):
        z = hz_ref[rows, :]
        lgf, kf = gates(z[:, :HGRN_QK_W])
        lgb, kb = gates(z[:, HGRN_QK_W:])
        return hq_ref[rows, :], kf, kb, lgf, lgb

    hgrn = _make_scan(hgrn_tile, hv_ref, hg_ref, jnp.concatenate([hnw_ref[...]] * HGRN_HEADS, axis=1),
                      tril_ref, triu_ref, hbd_ref, yc_ref, *scratch[N_SCAN_SCRATCH:],
                      tile=tile, heads=HGRN_HEADS, dk=HGRN_DK, dv=HGRN_DV)
    _run_scans([gla, hgrn], seq_len // tile)


def _scan_scratch(seq_len, tile, heads, dk, dv):
    w = heads * dk
    n_vt = heads * dv // LANES
    return [pltpu.VMEM((tile, w), F32),
            pltpu.VMEM((tile, w), F32),
            pltpu.VMEM((seq_len, heads * dv), F32),
            pltpu.VMEM((seq_len, w), BF16),
            pltpu.VMEM((seq_len, w), BF16),
            pltpu.VMEM((seq_len // tile, 8, w), F32),
            pltpu.VMEM((n_vt, LANES, LANES), F32),
            pltpu.VMEM((n_vt, LANES, LANES), F32)]


def _scans(aq, ak, alg, av, hq, hz, hi, g, gla_nw, hgrn_nw, lb_logits, tri, bd_gla, bd_hgrn,
           batch, seq_len, layer):
    n = batch * seq_len
    const = lambda b: (0, 0)
    seq = lambda wd, col=0: pl.BlockSpec((seq_len, wd), lambda b: (b, col))
    whole = lambda arr: pl.BlockSpec(arr.shape, const)
    scratch = (_scan_scratch(seq_len, SCAN_TILE, GLA_HEADS, GLA_DK, GLA_DV)
               + _scan_scratch(seq_len, SCAN_TILE, HGRN_HEADS, HGRN_DK, HGRN_DV))
    assert len(scratch) == 2 * N_SCAN_SCRATCH
    return pl.pallas_call(
        functools.partial(_scans_kernel, seq_len=seq_len, tile=SCAN_TILE, layer=layer),
        grid=(batch,),
        in_specs=[seq(GLA_QK_W), seq(GLA_QK_W), seq(2 * GLA_QK_W), seq(GLA_W), seq(GLA_W, 0),
                  whole(gla_nw), whole(bd_gla),
                  seq(HGRN_QK_W), seq(2 * HGRN_QK_W), whole(lb_logits), seq(HGRN_W),
                  seq(HGRN_W, (GLA_W + DIFF_W) // HGRN_W), whole(hgrn_nw), whole(bd_hgrn),
                  whole(tri[0]), whole(tri[1])],
        out_specs=[seq(GLA_W), seq(HGRN_W)],
        out_shape=[jax.ShapeDtypeStruct((n, GLA_W), BF16), jax.ShapeDtypeStruct((n, HGRN_W), BF16)],
        scratch_shapes=scratch,
        compiler_params=pltpu.CompilerParams(dimension_semantics=("parallel",),
                                             vmem_limit_bytes=VMEM_LIMIT),
        name="scans",
    )(aq, ak, alg, av, g, gla_nw, bd_gla, hq, hz, lb_logits, hi, g, hgrn_nw, bd_hgrn, tri[0], tri[1])


def _outproj_kernel(x_ref, ya_ref, yb_ref, yc_ref, w_ref, nw_ref, o_ref):
    y = (_dot(ya_ref[...], w_ref[0:GLA_W, :])
         + _dot(yb_ref[...], w_ref[GLA_W:GLA_W + DIFF_W, :])
         + _dot(yc_ref[...], w_ref[GLA_W + DIFF_W:, :]))
    y = y * lax.rsqrt(jnp.mean(y * y, axis=-1, keepdims=True) + EPS) * nw_ref[...]
    o_ref[...] = x_ref[...] + y


def _outproj(xf, ya, yb, yc, w, nw):
    n = xf.shape[0]
    tm = OUT_ROW_TILE
    row = lambda i: (i, 0)
    const = lambda i: (0, 0)
    return pl.pallas_call(
        _outproj_kernel,
        grid=(n // tm,),
        in_specs=[pl.BlockSpec((tm, D_MODEL), row),
                  pl.BlockSpec((tm, GLA_W), row),
                  pl.BlockSpec((tm, DIFF_W), row),
                  pl.BlockSpec((tm, HGRN_W), row),
                  pl.BlockSpec((MIX_W, D_MODEL), const),
                  pl.BlockSpec((1, D_MODEL), const)],
        out_specs=pl.BlockSpec((tm, D_MODEL), row),
        out_shape=jax.ShapeDtypeStruct((n, D_MODEL), F32),
        compiler_params=pltpu.CompilerParams(dimension_semantics=("parallel",),
                                             vmem_limit_bytes=VMEM_LIMIT),
        name="outproj",
    )(xf, ya, yb, yc, w, nw)


def _block_mean_matrix(heads, dv):
    m = np.kron(np.eye(heads, dtype=np.float32), np.full((dv, dv), 1.0 / dv, np.float32))
    return jnp.asarray(m, BF16)


def kernel(x, norm_pre, norm_post, w_in, w_out, gla_wa2_fwd, gla_ba_fwd, gla_wa2_bwd, gla_ba_bwd, gla_norm,
           diff_lq1, diff_lk1, diff_lq2, diff_lk2, diff_norm, hgrn_lb_logits, hgrn_norm):
    batch, seq_len, d_model = x.shape
    assert d_model == D_MODEL and w_in.shape == (DEPTH, D_MODEL, IN_W)
    assert seq_len % ROW_TILE == 0 and seq_len % Q_TILE == 0 and seq_len % SCAN_TILE == 0
    assert seq_len % KEY_CHUNK == 0 and (batch * seq_len) % OUT_ROW_TILE == 0
    n = batch * seq_len
    xf = x.reshape(n, D_MODEL)

    a0 = 2 * GLA_QK_W + 2 * GLA_W
    w_in_b = w_in.astype(BF16)
    w_perm = jnp.concatenate(
        [w_in_b[:, :, :a0], w_in_b[:, :, a0 + 2 * GLA_RANK:], w_in_b[:, :, a0:a0 + 2 * GLA_RANK],
         jnp.zeros((DEPTH, D_MODEL, A_PAD - 2 * GLA_RANK), BF16)], axis=-1)
    wa2 = jnp.zeros((DEPTH, A_PAD, 2 * GLA_QK_W), F32)
    wa2 = wa2.at[:, :GLA_RANK, :GLA_QK_W].set(gla_wa2_fwd)
    wa2 = wa2.at[:, GLA_RANK:2 * GLA_RANK, GLA_QK_W:].set(gla_wa2_bwd).astype(BF16)
    ba = jnp.concatenate([gla_ba_fwd, gla_ba_bwd], axis=-1)
    w_out_b = w_out.astype(BF16)

    inv_freq = ROPE_THETA ** (-jnp.arange(0, DIFF_HD, 2, dtype=F32) / DIFF_HD)
    ang = jnp.arange(seq_len, dtype=jnp.int32).astype(F32)[:, None] * inv_freq[None, :]
    cos_t = jnp.tile(jnp.cos(ang), (1, 2 * LANES // DIFF_HD))
    sin_t = jnp.tile(jnp.concatenate([-jnp.sin(ang), jnp.sin(ang)], axis=-1), (1, LANES // DIFF_HD))

    tri = _tri_matrices(SCAN_TILE)
    bd_gla = _block_mean_matrix(GLA_HEADS, GLA_DV)
    bd_hgrn = _block_mean_matrix(HGRN_HEADS, HGRN_DV)

    for layer in range(DEPTH):
        lambda_init = 0.8 - 0.6 * math.exp(-0.3 * layer)
        (aq, ak, av, alg, dq, dk, dv, hq, hz, hi, g) = _inproj(
            xf, norm_pre[layer][None, :], w_perm[layer], wa2[layer], ba[layer][None, :], cos_t, sin_t, seq_len)
        yb = _attention(dq, dk, dv, g, diff_norm[layer][None, :], diff_lq1[layer][None, :],
                        diff_lk1[layer][None, :], diff_lq2[layer][None, :], diff_lk2[layer][None, :],
                        batch, seq_len, lambda_init)
        ya, yc = _scans(aq, ak, alg, av, hq, hz, hi, g, gla_norm[layer][None, :], hgrn_norm[layer][None, :],
                        hgrn_lb_logits, tri, bd_gla, bd_hgrn, batch, seq_len, layer)
        xf = _outproj(xf, ya, yb, yc, w_out_b[layer], norm_post[layer][None, :])
    return xf.reshape(batch, seq_len, D_MODEL)
```

```python
import functools
import math

import numpy as np
import jax
import jax.numpy as jnp
from jax import lax
from jax.experimental import pallas as pl
from jax.experimental.pallas import tpu as pltpu

F32 = jnp.float32
BF16 = jnp.bfloat16

D_MODEL = 1024
DEPTH = 4
GLA_HEADS, GLA_DK, GLA_DV, GLA_RANK = 4, 32, 64, 16
GLA_GATE_TEMP = 16.0
DIFF_HEADS, DIFF_HD = 4, 64
DIFF_DV = 2 * DIFF_HD
HGRN_HEADS, HGRN_DK, HGRN_DV = 4, 64, 64
ROPE_THETA = 10000.0
EPS = 1e-6
F_FLOOR = 1e-30
LOG2E = math.log2(math.e)

GLA_QK_W = GLA_HEADS * GLA_DK
GLA_W = GLA_HEADS * GLA_DV
DIFF_QK_W = DIFF_HEADS * 2 * DIFF_HD
DIFF_W = DIFF_HEADS * DIFF_DV
HGRN_QK_W = HGRN_HEADS * HGRN_DK
HGRN_W = HGRN_HEADS * HGRN_DV
MIX_W = GLA_W + DIFF_W + HGRN_W
IN_W = 2 * GLA_QK_W + 2 * GLA_W + 2 * GLA_RANK + 2 * DIFF_QK_W + 2 * DIFF_W + 3 * HGRN_QK_W + 2 * HGRN_W

LANES = 128
A_PAD = LANES
C_GLA = 0
C_DIFF = C_GLA + 2 * GLA_QK_W + 2 * GLA_W
C_HGRN = C_DIFF + 2 * DIFF_QK_W + 2 * DIFF_W
C_A = C_HGRN + 3 * HGRN_QK_W + 2 * HGRN_W
IN_WP = C_A + A_PAD

VMEM_LIMIT = 56 * 1024 * 1024

ROW_TILE = 512
OUT_ROW_TILE = 512
Q_TILE = 256
SCAN_TILE = 128


def _dot(a, b):
    return jnp.dot(a, b, preferred_element_type=F32)


def _dot_nt(a, b):
    return lax.dot_general(a, b, (((1,), (1,)), ((), ())), preferred_element_type=F32)


def _dot_tn(a, b):
    return lax.dot_general(a, b, (((0,), (0,)), ((), ())), preferred_element_type=F32)


def _split_bf16(x):
    hi = x.astype(BF16)
    lo = (x - hi.astype(F32)).astype(BF16)
    return hi, lo


def _sigmoid(x):
    return 1.0 / (1.0 + jnp.exp(-x))


def _silu(x):
    return x * _sigmoid(x)


def _inproj_kernel(x_ref, *refs):
    _inproj_body(x_ref[...], *refs)


def _inproj_body(x, nw_ref, w_ref, wa2_ref, ba_ref, cos_ref, sin_ref,
                 aq_ref, ak_ref, av_ref, alg_ref, dq_ref, dk_ref, dv_ref,
                 hq_ref, hz_ref, hi_ref, g_ref):
    h = x * lax.rsqrt(jnp.mean(x * x, axis=-1, keepdims=True) + EPS) * nw_ref[...]
    hb = h.astype(BF16)

    def proj(c0, width):
        return _dot(hb, w_ref[:, c0:c0 + width])

    p = proj(C_GLA, 2 * GLA_QK_W + 2 * GLA_W)
    aq_ref[...] = (p[:, :GLA_QK_W] * (GLA_DK ** -0.5)).astype(BF16)
    ak_ref[...] = p[:, GLA_QK_W:2 * GLA_QK_W].astype(BF16)
    av_ref[...] = p[:, 2 * GLA_QK_W:2 * GLA_QK_W + GLA_W].astype(BF16)
    g_ref[:, 0:GLA_W] = p[:, 2 * GLA_QK_W + GLA_W:]
    a = proj(C_A, A_PAD)
    zz = _dot(a.astype(BF16), wa2_ref[...]) + ba_ref[...]
    alg_ref[...] = (jnp.minimum(zz, 0.0) - jnp.log1p(jnp.exp(-jnp.abs(zz)))) * (LOG2E / GLA_GATE_TEMP)

    cos = cos_ref[...]
    sin = sin_ref[...]
    lane = lax.broadcasted_iota(jnp.int32, (1, LANES), 1)
    first_half = (lane % DIFF_HD) < (DIFF_HD // 2)

    def rope_store(c0, out_ref, scale):
        pq = proj(c0, DIFF_QK_W)
        for j in range(DIFF_QK_W // LANES):
            xs = pq[:, j * LANES:(j + 1) * LANES]
            partner = jnp.where(first_half,
                                pltpu.roll(xs, LANES - DIFF_HD // 2, 1),
                                pltpu.roll(xs, DIFF_HD // 2, 1))
            r = xs * cos + partner * sin
            if scale != 1.0:
                r = r * scale
            out_ref[:, j * LANES:(j + 1) * LANES] = r.astype(BF16)

    rope_store(C_DIFF, dq_ref, DIFF_HD ** -0.5 * LOG2E)
    rope_store(C_DIFF + DIFF_QK_W, dk_ref, 1.0)
    p = proj(C_DIFF + 2 * DIFF_QK_W, 2 * DIFF_W)
    dv_ref[...] = p[:, :DIFF_W].astype(BF16)
    g_ref[:, GLA_W:GLA_W + DIFF_W] = p[:, DIFF_W:]

    p = proj(C_HGRN, 3 * HGRN_QK_W + 2 * HGRN_W)
    hq_ref[...] = p[:, :HGRN_QK_W].astype(BF16)
    hz_ref[...] = p[:, HGRN_QK_W:3 * HGRN_QK_W]
    hi_ref[...] = p[:, 3 * HGRN_QK_W:3 * HGRN_QK_W + HGRN_W].astype(BF16)
    g_ref[:, GLA_W + DIFF_W:] = p[:, 3 * HGRN_QK_W + HGRN_W:]


def _inproj(xf, nw, w, wa2, ba, cos_t, sin_t, seq_len, prev=None):
    n = xf.shape[0]
    tm = ROW_TILE
    n_pos_tiles = seq_len // tm
    row = lambda i: (i, 0)
    const = lambda i: (0, 0)
    pos = lambda i: (i % n_pos_tiles, 0)
    widths = [(GLA_QK_W, BF16), (GLA_QK_W, BF16), (GLA_W, BF16), (2 * GLA_QK_W, F32),
              (DIFF_QK_W, BF16), (DIFF_QK_W, BF16), (DIFF_W, BF16),
              (HGRN_QK_W, BF16), (2 * HGRN_QK_W, F32), (HGRN_W, BF16), (MIX_W, F32)]
    in_specs = [pl.BlockSpec((1, D_MODEL), const),
                pl.BlockSpec((D_MODEL, IN_WP), const, pipeline_mode=pl.Buffered(1)),
                pl.BlockSpec((A_PAD, 2 * GLA_QK_W), const),
                pl.BlockSpec((1, 2 * GLA_QK_W), const),
                pl.BlockSpec((tm, LANES), pos),
                pl.BlockSpec((tm, LANES), pos)]
    operands = (nw, w, wa2, ba, cos_t, sin_t)
    if prev is None:
        body, name = _inproj_kernel, "inproj"
        lead_specs, lead = [pl.BlockSpec((tm, D_MODEL), row)], (xf,)
    else:
        body, name = _out_in_kernel, "outproj_inproj"
        widths = [(D_MODEL, F32)] + widths
        lead_specs = [pl.BlockSpec((tm, D_MODEL), row),
                      pl.BlockSpec((tm, GLA_W), row),
                      pl.BlockSpec((tm, DIFF_W), row),
                      pl.BlockSpec((tm, HGRN_W), row),
                      pl.BlockSpec((MIX_W, D_MODEL), const, pipeline_mode=pl.Buffered(1)),
                      pl.BlockSpec((1, D_MODEL), const)]
        lead = (xf,) + tuple(prev)
    return pl.pallas_call(
        body,
        grid=(n // tm,),
        in_specs=lead_specs + in_specs,
        out_specs=[pl.BlockSpec((tm, wd), row) for wd, _ in widths],
        out_shape=[jax.ShapeDtypeStruct((n, wd), dt) for wd, dt in widths],
        compiler_params=pltpu.CompilerParams(dimension_semantics=("parallel",),
                                             vmem_limit_bytes=VMEM_LIMIT),
        name=name,
    )(*lead, *operands)


ONES_ROWS = 16
KEY_CHUNK = 512
SCORE_LOOKAHEAD = 6


def _attn_kernel(q_ref, k_ref, v_ref, g_ref, nw_ref, lq1_ref, lk1_ref, lq2_ref, lk2_ref, o_ref, vt_scr,
                 *, lambda_init):
    seq_len = k_ref.shape[0]
    vt_scr[0:DIFF_DV, :] = v_ref[...].astype(F32).T.astype(BF16)
    vt_scr[DIFF_DV:, :] = jnp.ones((ONES_ROWS, seq_len), BF16)

    lam = (jnp.exp(jnp.sum(lq1_ref[...] * lk1_ref[...], axis=-1, keepdims=True))
           - jnp.exp(jnp.sum(lq2_ref[...] * lk2_ref[...], axis=-1, keepdims=True)) + lambda_init)
    lane = lax.broadcasted_iota(jnp.int32, (1, 2 * DIFF_HD), 1)
    nk = seq_len // KEY_CHUNK
    items = [(qi, c, j) for qi in range(seq_len // Q_TILE) for c in range(2) for j in range(nk)]
    qmap_cache = {}

    def qmap(qi, c):
        if (qi, c) not in qmap_cache:
            q = q_ref[qi * Q_TILE:(qi + 1) * Q_TILE, :]
            qmap_cache[(qi, c)] = jnp.where((lane // DIFF_HD) == c, q, jnp.zeros_like(q))
        return qmap_cache[(qi, c)]

    def scores(item):
        qi, c, j = item
        return _dot_nt(k_ref[j * KEY_CHUNK:(j + 1) * KEY_CHUNK, :], qmap(qi, c))

    ahead = [scores(it) for it in items[:SCORE_LOOKAHEAD]]
    parts, maxes, map_out = [], [], []
    for n, (qi, c, j) in enumerate(items):
        st = ahead.pop(0)
        if n + SCORE_LOOKAHEAD < len(items):
            ahead.append(scores(items[n + SCORE_LOOKAHEAD]))
        m = jnp.max(st, axis=0, keepdims=True)
        e = jnp.exp2(st - m).astype(BF16)
        parts.append(_dot(vt_scr[:, j * KEY_CHUNK:(j + 1) * KEY_CHUNK], e))
        maxes.append(m)
        if j < nk - 1:
            continue
        m_all = functools.reduce(jnp.maximum, maxes)
        tot = parts[0] * jnp.exp2(maxes[0] - m_all)
        for part, mj in zip(parts[1:], maxes[1:]):
            tot = tot + part * jnp.exp2(mj - m_all)
        map_out.append(tot[0:DIFF_DV, :] * (1.0 / tot[DIFF_DV:DIFF_DV + 1, :]))
        parts, maxes = [], []
        if c == 0:
            continue
        rows = slice(qi * Q_TILE, (qi + 1) * Q_TILE)
        o = (map_out[0] - lam * map_out[1]).T
        map_out = []
        o = o * lax.rsqrt(jnp.mean(o * o, axis=-1, keepdims=True) + EPS) * nw_ref[...] * (1.0 - lambda_init)
        o_ref[rows, :] = (o * _silu(g_ref[rows, :])).astype(BF16)


def _attention(dq, dk, dv, g, nw, lq1, lk1, lq2, lk2, batch, seq_len, lambda_init):
    n = dq.shape[0]
    bh = lambda b, h: (b, h)
    gmap = lambda b, h: (b, GLA_W // DIFF_DV + h)
    const = lambda b, h: (0, 0)
    small = pl.BlockSpec((1, DIFF_HD), const)
    return pl.pallas_call(
        functools.partial(_attn_kernel, lambda_init=lambda_init),
        grid=(batch, DIFF_HEADS),
        in_specs=[pl.BlockSpec((seq_len, 2 * DIFF_HD), bh),
                  pl.BlockSpec((seq_len, 2 * DIFF_HD), bh),
                  pl.BlockSpec((seq_len, DIFF_DV), bh),
                  pl.BlockSpec((seq_len, DIFF_DV), gmap),
                  pl.BlockSpec((1, DIFF_DV), const),
                  small, small, small, small],
        out_specs=pl.BlockSpec((seq_len, DIFF_DV), bh),
        out_shape=jax.ShapeDtypeStruct((n, DIFF_W), BF16),
        scratch_shapes=[pltpu.VMEM((DIFF_DV + ONES_ROWS, seq_len), BF16)],
        compiler_params=pltpu.CompilerParams(dimension_semantics=("parallel", "parallel"),
                                             vmem_limit_bytes=VMEM_LIMIT),
        name="diff_attn",
    )(dq, dk, dv, g, nw, lq1, lk1, lq2, lk2)


def _scan_levels(tile):
    nlev = int(math.log2(tile))
    assert 1 << nlev == tile
    return [tile >> (j + 1) for j in range(nlev)]


def _tri_matrices(tile):
    idx = np.arange(tile)
    tril = (idx[None, :] <= idx[:, None]).astype(np.float32)
    return jnp.asarray(tril, BF16), jnp.asarray(tril.T, BF16)


def _interleave(generators):
    live = list(generators)
    while live:
        for gen in list(live):
            try:
                next(gen)
            except StopIteration:
                live.remove(gen)


def _run_scans(scans, n_tiles):
    for init, _, _ in scans:
        init()

    def fwd(i, carry):
        _interleave([steps(i) for _, steps, _ in scans])
        return carry

    lax.fori_loop(0, n_tiles, fwd, 0, unroll=2)

    def bwd(n, carry):
        _interleave([steps(n_tiles - 1 - n) for _, _, steps in scans])
        return carry

    lax.fori_loop(0, n_tiles, bwd, 0, unroll=4)


def _make_scan(load_tile, v_ref, g_ref, nw, tril_ref, triu_ref, bd_ref, y_ref,
               pf_scr, sb_scr, o_scr, qb_scr, kb_scr, totb_scr, stf_scr, stb_scr,
               *, tile, heads, dk, dv):
    t_ = tile
    levels = _scan_levels(t_)
    w = heads * dk
    n_vt = heads * dv // LANES
    hpv = LANES // dv
    sub8 = lax.broadcasted_iota(jnp.int32, (8, 1), 0)
    lane = lax.broadcasted_iota(jnp.int32, (1, LANES), 1)
    xr = (lax.broadcasted_iota(jnp.int32, (t_, hpv * t_), 0)
          ^ (lax.broadcasted_iota(jnp.int32, (t_, hpv * t_), 1) & (t_ - 1)))
    vrow = lax.broadcasted_iota(jnp.int32, (LANES, LANES), 0)
    kcol = lax.broadcasted_iota(jnp.int32, (LANES, LANES), 1)

    def tile_info(p):
        first = p * hpv * dk
        cols = slice((first // LANES) * LANES, (first // LANES + 1) * LANES)
        off = first % LANES
        kmasks = [(lane >= off + r * dk) & (lane < off + (r + 1) * dk) for r in range(hpv)]
        vmasks = [(lane >= r * dv) & (lane < (r + 1) * dv) for r in range(hpv)]
        valid = (vrow // dv) == ((kcol - off) // dk)
        valid = valid & (kcol >= off) & (kcol < off + hpv * dk)
        return cols, kmasks, vmasks, valid

    def stack_heads(x, masks):
        return jnp.concatenate([jnp.where(mk, x, jnp.zeros_like(x)) for mk in masks], axis=0)

    def init():
        stf_scr[...] = jnp.zeros_like(stf_scr)
        stb_scr[...] = jnp.zeros_like(stb_scr)

    def fwd_steps(i):
        r0 = pl.multiple_of(i * t_, t_)
        rows = pl.ds(r0, t_)
        q, kf, kb, lgf, lgb = load_tile(rows)
        hi_f, lo_f = _split_bf16(lgf)
        hi_b, lo_b = _split_bf16(lgb)
        pp = _dot(tril_ref[...], jnp.concatenate([hi_f, lo_f], axis=1))
        ss = _dot(triu_ref[...], jnp.concatenate([hi_b, lo_b], axis=1))
        pf = pp[:, :w] + pp[:, w:]
        sb = ss[:, :w] + ss[:, w:]
        pf_scr[...] = pf
        sb_scr[...] = sb
        yield
        row_cache = {}

        def brow(scr, r):
            key = (id(scr), r)
            if key not in row_cache:
                row_cache[key] = jnp.broadcast_to(scr[r:r + 1, :], (8, w))
            return row_cache[key]

        def level_exponents(m):
            tgt, src = [], []
            for j in range(t_ // 8):
                r8 = slice(8 * j, 8 * j + 8)
                pf_t = pf_scr[r8, :]
                sb_t = sb_scr[r8, :]
                if m >= 8:
                    mid = (8 * j // (2 * m)) * 2 * m + m
                    gf, gb = brow(pf_scr, mid - 1), brow(sb_scr, mid)
                    if (8 * j // m) % 2 == 0:
                        tgt.append(sb_t - gb)
                        src.append(gf - pf_t)
                    else:
                        tgt.append(pf_t - gf)
                        src.append(gb - sb_t)
                    continue
                odd = (sub8 & m) != 0
                if m == 1:
                    tgt.append(jnp.where(odd, lgf[r8], lgb[r8]))
                    continue
                if m == 4:
                    gf, gb = brow(pf_scr, 8 * j + 3), brow(sb_scr, 8 * j + 4)
                else:
                    low = sub8 < 4
                    gf = jnp.where(low, brow(pf_scr, 8 * j + 1), brow(pf_scr, 8 * j + 5))
                    gb = jnp.where(low, brow(sb_scr, 8 * j + 2), brow(sb_scr, 8 * j + 6))
                d_f = pf_t - gf
                d_b = sb_t - gb
                tgt.append(jnp.where(odd, d_f, d_b))
                src.append(-jnp.where(odd, d_b, d_f))
            return tgt, src

        def pow2_16(pieces):
            return jnp.exp2(jnp.concatenate(pieces, axis=0)).astype(BF16)

        q16 = q.astype(BF16)
        kf16 = kf.astype(BF16)
        kb16 = kb.astype(BF16)
        row = lax.broadcasted_iota(jnp.int32, (t_, 1), 0)
        info = [tile_info(p) for p in range(n_vt)]
        acc = [None] * n_vt
        for li, m in enumerate(levels):
            tgt, src = level_exponents(m)
            lhs = q16 * pow2_16(tgt)
            if m >= 16:
                kk = jnp.concatenate([(kb16 if (r // m) % 2 else kf16)[r:r + m] for r in range(0, t_, m)], axis=0)
            else:
                kk = jnp.where((row & m) != 0, kb16, kf16)
            rhs = kk if m == 1 else kk * pow2_16(src)
            yield
            for p, (cols, kmasks, _, _) in enumerate(info):
                a = _dot_nt(lhs[:, cols], stack_heads(rhs[:, cols], kmasks))
                acc[p] = a if li == 0 else jnp.where(xr < 2 * m, a, acc[p])
                yield
        ksum16 = (kf + kb).astype(BF16)
        for p, (cols, kmasks, _, _) in enumerate(info):
            acc[p] = jnp.where(xr == 0, _dot_nt(q16[:, cols], stack_heads(ksum16[:, cols], kmasks)), acc[p])
        yield
        totf = pf_scr[t_ - 1:t_, :]
        totb = sb_scr[0:1, :]
        qf = q16 * jnp.exp2(pf).astype(BF16)
        kfd = kf16 * jnp.exp2(totf - pf).astype(BF16)
        yield
        qb_scr[rows, :] = q16 * jnp.exp2(sb).astype(BF16)
        kb_scr[rows, :] = kb16 * jnp.exp2(totb - sb).astype(BF16)
        dec_f = jnp.exp2(totf)
        totb_scr[i] = jnp.broadcast_to(jnp.exp2(totb), (8, w))
        yield

        v = v_ref[rows, :]
        outs = []
        for p, (cols, _, vmasks, valid) in enumerate(info):
            v_p = v[:, p * LANES:(p + 1) * LANES]
            st = stf_scr[p]
            outs.append(_dot(acc[p].astype(BF16), stack_heads(v_p, vmasks))
                        + _dot_nt(qf[:, cols], st.astype(BF16)))
            stf_scr[p] = jnp.where(valid, st * dec_f[:, cols] + _dot_tn(v_p, kfd[:, cols]), 0.0)
            yield
        o_scr[rows, :] = jnp.concatenate(outs, axis=1)

    def bwd_steps(i):
        r0 = pl.multiple_of(i * t_, t_)
        rows = pl.ds(r0, t_)
        v = v_ref[rows, :]
        qb = qb_scr[rows, :]
        kbd = kb_scr[rows, :]
        dec_b = totb_scr[i][0:1, :]
        outs = []
        for p in range(n_vt):
            cols, _, _, valid = tile_info(p)
            st = stb_scr[p]
            outs.append(_dot_nt(qb[:, cols], st.astype(BF16)))
            stb_scr[p] = jnp.where(valid, st * dec_b[:, cols] + _dot_tn(v[:, p * LANES:(p + 1) * LANES],
                                                                       kbd[:, cols]), 0.0)
            yield
        o = o_scr[rows, :] + jnp.concatenate(outs, axis=1)
        hi, lo = _split_bf16(o * o)
        bd = bd_ref[...]
        ms = _dot(hi, bd) + _dot(lo, bd)
        yield
        y = o * lax.rsqrt(ms + EPS) * nw * _silu(g_ref[rows, :])
        y_ref[rows, :] = y.astype(BF16)

    return init, fwd_steps, bwd_steps


N_SCAN_SCRATCH = 8


def _scans_kernel(aq_ref, ak_ref, alg_ref, av_ref, ag_ref, anw_ref, abd_ref,
                  hq_ref, hz_ref, lbl_ref, hv_ref, hg_ref, hnw_ref, hbd_ref, tril_ref, triu_ref,
                  ya_ref, yc_ref, *scratch, seq_len, tile, layer):
    def gla_tile(rows):
        k = ak_ref[rows, :]
        lg = alg_ref[rows, :]
        return aq_ref[rows, :], k, k, lg[:, :GLA_QK_W], lg[:, GLA_QK_W:]

    gla = _make_scan(gla_tile, av_ref, ag_ref, jnp.concatenate([anw_ref[...]] * GLA_HEADS, axis=1),
                     tril_ref, triu_ref, abd_ref, ya_ref, *scratch[:N_SCAN_SCRATCH],
                     tile=tile, heads=GLA_HEADS, dk=GLA_DK, dv=GLA_DV)

    logits = lbl_ref[...]
    e = jnp.exp(logits - jnp.max(logits, axis=0, keepdims=True))
    share = e / jnp.sum(e, axis=0, keepdims=True)
    lb = jnp.zeros((1, HGRN_QK_W), F32)
    for j in range(1, layer + 1):
        lb = lb + share[j:j + 1, :]

    def gates(z):
        t = jnp.exp(-jnp.abs(z))
        r = 1.0 / (1.0 + t)
        pos = z >= 0.0
        sig = jnp.where(pos, r, t * r)
        sig_neg = jnp.where(pos, t * r, r)
        f = lb + (1.0 - lb) * sig
        return jnp.log(jnp.maximum(f, F_FLOOR)) * LOG2E, (1.0 - lb) * sig_neg

    def hgrn_tile(rows):
        z = hz_ref[rows, :]
        lgf, kf = gates(z[:, :HGRN_QK_W])
        lgb, kb = gates(z[:, HGRN_QK_W:])
        return hq_ref[rows, :], kf, kb, lgf, lgb

    hgrn = _make_scan(hgrn_tile, hv_ref, hg_ref, jnp.concatenate([hnw_ref[...]] * HGRN_HEADS, axis=1),
                      tril_ref, triu_ref, hbd_ref, yc_ref, *scratch[N_SCAN_SCRATCH:],
                      tile=tile, heads=HGRN_HEADS, dk=HGRN_DK, dv=HGRN_DV)
    _run_scans([gla, hgrn], seq_len // tile)


def _scan_scratch(seq_len, tile, heads, dk, dv):
    w = heads * dk
    n_vt = heads * dv // LANES
    return [pltpu.VMEM((tile, w), F32),
            pltpu.VMEM((tile, w), F32),
            pltpu.VMEM((seq_len, heads * dv), F32),
            pltpu.VMEM((seq_len, w), BF16),
            pltpu.VMEM((seq_len, w), BF16),
            pltpu.VMEM((seq_len // tile, 8, w), F32),
            pltpu.VMEM((n_vt, LANES, LANES), F32),
            pltpu.VMEM((n_vt, LANES, LANES), F32)]


def _scans(aq, ak, alg, av, hq, hz, hi, g, gla_nw, hgrn_nw, lb_logits, tri, bd_gla, bd_hgrn,
           batch, seq_len, layer):
    n = batch * seq_len
    const = lambda b: (0, 0)
    seq = lambda wd, col=0: pl.BlockSpec((seq_len, wd), lambda b: (b, col))
    whole = lambda arr: pl.BlockSpec(arr.shape, const)
    scratch = (_scan_scratch(seq_len, SCAN_TILE, GLA_HEADS, GLA_DK, GLA_DV)
               + _scan_scratch(seq_len, SCAN_TILE, HGRN_HEADS, HGRN_DK, HGRN_DV))
    assert len(scratch) == 2 * N_SCAN_SCRATCH
    return pl.pallas_call(
        functools.partial(_scans_kernel, seq_len=seq_len, tile=SCAN_TILE, layer=layer),
        grid=(batch,),
        in_specs=[seq(GLA_QK_W), seq(GLA_QK_W), seq(2 * GLA_QK_W), seq(GLA_W), seq(GLA_W, 0),
                  whole(gla_nw), whole(bd_gla),
                  seq(HGRN_QK_W), seq(2 * HGRN_QK_W), whole(lb_logits), seq(HGRN_W),
                  seq(HGRN_W, (GLA_W + DIFF_W) // HGRN_W), whole(hgrn_nw), whole(bd_hgrn),
                  whole(tri[0]), whole(tri[1])],
        out_specs=[seq(GLA_W), seq(HGRN_W)],
        out_shape=[jax.ShapeDtypeStruct((n, GLA_W), BF16), jax.ShapeDtypeStruct((n, HGRN_W), BF16)],
        scratch_shapes=scratch,
        compiler_params=pltpu.CompilerParams(dimension_semantics=("parallel",),
                                             vmem_limit_bytes=VMEM_LIMIT),
        name="scans",
    )(aq, ak, alg, av, g, gla_nw, bd_gla, hq, hz, lb_logits, hi, g, hgrn_nw, bd_hgrn, tri[0], tri[1])


def _outproj_value(x_ref, ya_ref, yb_ref, yc_ref, w_ref, nw_ref):
    y = (_dot(ya_ref[...], w_ref[0:GLA_W, :])
         + _dot(yb_ref[...], w_ref[GLA_W:GLA_W + DIFF_W, :])
         + _dot(yc_ref[...], w_ref[GLA_W + DIFF_W:, :]))
    y = y * lax.rsqrt(jnp.mean(y * y, axis=-1, keepdims=True) + EPS) * nw_ref[...]
    return x_ref[...] + y


def _outproj_kernel(x_ref, ya_ref, yb_ref, yc_ref, w_ref, nw_ref, o_ref):
    o_ref[...] = _outproj_value(x_ref, ya_ref, yb_ref, yc_ref, w_ref, nw_ref)


def _out_in_kernel(x_ref, ya_ref, yb_ref, yc_ref, wout_ref, npost_ref,
                   npre_ref, w_ref, wa2_ref, ba_ref, cos_ref, sin_ref, xnew_ref, *outs):
    x = _outproj_value(x_ref, ya_ref, yb_ref, yc_ref, wout_ref, npost_ref)
    xnew_ref[...] = x
    _inproj_body(x, npre_ref, w_ref, wa2_ref, ba_ref, cos_ref, sin_ref, *outs)


def _outproj(xf, ya, yb, yc, w, nw):
    n = xf.shape[0]
    tm = OUT_ROW_TILE
    row = lambda i: (i, 0)
    const = lambda i: (0, 0)
    return pl.pallas_call(
        _outproj_kernel,
        grid=(n // tm,),
        in_specs=[pl.BlockSpec((tm, D_MODEL), row),
                  pl.BlockSpec((tm, GLA_W), row),
                  pl.BlockSpec((tm, DIFF_W), row),
                  pl.BlockSpec((tm, HGRN_W), row),
                  pl.BlockSpec((MIX_W, D_MODEL), const),
                  pl.BlockSpec((1, D_MODEL), const)],
        out_specs=pl.BlockSpec((tm, D_MODEL), row),
        out_shape=jax.ShapeDtypeStruct((n, D_MODEL), F32),
        compiler_params=pltpu.CompilerParams(dimension_semantics=("parallel",),
                                             vmem_limit_bytes=VMEM_LIMIT),
        name="outproj",
    )(xf, ya, yb, yc, w, nw)


def _block_mean_matrix(heads, dv):
    m = np.kron(np.eye(heads, dtype=np.float32), np.full((dv, dv), 1.0 / dv, np.float32))
    return jnp.asarray(m, BF16)


def kernel(x, norm_pre, norm_post, w_in, w_out, gla_wa2_fwd, gla_ba_fwd, gla_wa2_bwd, gla_ba_bwd, gla_norm,
           diff_lq1, diff_lk1, diff_lq2, diff_lk2, diff_norm, hgrn_lb_logits, hgrn_norm):
    batch, seq_len, d_model = x.shape
    assert d_model == D_MODEL and w_in.shape == (DEPTH, D_MODEL, IN_W)
    assert seq_len % ROW_TILE == 0 and seq_len % Q_TILE == 0 and seq_len % SCAN_TILE == 0
    assert seq_len % KEY_CHUNK == 0 and (batch * seq_len) % OUT_ROW_TILE == 0
    n = batch * seq_len
    xf = x.reshape(n, D_MODEL)

    a0 = 2 * GLA_QK_W + 2 * GLA_W
    w_in_b = w_in.astype(BF16)
    w_perm = jnp.concatenate(
        [w_in_b[:, :, :a0], w_in_b[:, :, a0 + 2 * GLA_RANK:], w_in_b[:, :, a0:a0 + 2 * GLA_RANK],
         jnp.zeros((DEPTH, D_MODEL, A_PAD - 2 * GLA_RANK), BF16)], axis=-1)
    wa2 = jnp.zeros((DEPTH, A_PAD, 2 * GLA_QK_W), F32)
    wa2 = wa2.at[:, :GLA_RANK, :GLA_QK_W].set(gla_wa2_fwd)
    wa2 = wa2.at[:, GLA_RANK:2 * GLA_RANK, GLA_QK_W:].set(gla_wa2_bwd).astype(BF16)
    ba = jnp.concatenate([gla_ba_fwd, gla_ba_bwd], axis=-1)
    w_out_b = w_out.astype(BF16)

    inv_freq = ROPE_THETA ** (-jnp.arange(0, DIFF_HD, 2, dtype=F32) / DIFF_HD)
    ang = jnp.arange(seq_len, dtype=jnp.int32).astype(F32)[:, None] * inv_freq[None, :]
    cos_t = jnp.tile(jnp.cos(ang), (1, 2 * LANES // DIFF_HD))
    sin_t = jnp.tile(jnp.concatenate([-jnp.sin(ang), jnp.sin(ang)], axis=-1), (1, LANES // DIFF_HD))

    tri = _tri_matrices(SCAN_TILE)
    bd_gla = _block_mean_matrix(GLA_HEADS, GLA_DV)
    bd_hgrn = _block_mean_matrix(HGRN_HEADS, HGRN_DV)

    prev = None
    for layer in range(DEPTH):
        lambda_init = 0.8 - 0.6 * math.exp(-0.3 * layer)
        outs = _inproj(xf, norm_pre[layer][None, :], w_perm[layer], wa2[layer], ba[layer][None, :],
                       cos_t, sin_t, seq_len, prev)
        if prev is not None:
            xf, outs = outs[0], outs[1:]
        (aq, ak, av, alg, dq, dk, dv, hq, hz, hi, g) = outs
        yb = _attention(dq, dk, dv, g, diff_norm[layer][None, :], diff_lq1[layer][None, :],
                        diff_lk1[layer][None, :], diff_lq2[layer][None, :], diff_lk2[layer][None, :],
                        batch, seq_len, lambda_init)
        ya, yc = _scans(aq, ak, alg, av, hq, hz, hi, g, gla_norm[layer][None, :], hgrn_norm[layer][None, :],
                        hgrn_lb_logits, tri, bd_gla, bd_hgrn, batch, seq_len, layer)
        prev = (ya, yb, yc, w_out_b[layer], norm_post[layer][None, :])
    xf = _outproj(xf, *prev)
    return xf.reshape(batch, seq_len, D_MODEL)
```

```python
import functools
import math

import numpy as np
import jax
import jax.numpy as jnp
from jax import lax
from jax.experimental import pallas as pl
from jax.experimental.pallas import tpu as pltpu

F32 = jnp.float32
BF16 = jnp.bfloat16

D_MODEL = 1024
DEPTH = 4
GLA_HEADS, GLA_DK, GLA_DV, GLA_RANK = 4, 32, 64, 16
GLA_GATE_TEMP = 16.0
DIFF_HEADS, DIFF_HD = 4, 64
DIFF_DV = 2 * DIFF_HD
HGRN_HEADS, HGRN_DK, HGRN_DV = 4, 64, 64
ROPE_THETA = 10000.0
EPS = 1e-6
F_FLOOR = 1e-30
LOG2E = math.log2(math.e)

GLA_QK_W = GLA_HEADS * GLA_DK
GLA_W = GLA_HEADS * GLA_DV
DIFF_QK_W = DIFF_HEADS * 2 * DIFF_HD
DIFF_W = DIFF_HEADS * DIFF_DV
HGRN_QK_W = HGRN_HEADS * HGRN_DK
HGRN_W = HGRN_HEADS * HGRN_DV
MIX_W = GLA_W + DIFF_W + HGRN_W
IN_W = 2 * GLA_QK_W + 2 * GLA_W + 2 * GLA_RANK + 2 * DIFF_QK_W + 2 * DIFF_W + 3 * HGRN_QK_W + 2 * HGRN_W

LANES = 128
A_PAD = LANES
C_GLA = 0
C_DIFF = C_GLA + 2 * GLA_QK_W + 2 * GLA_W
C_HGRN = C_DIFF + 2 * DIFF_QK_W + 2 * DIFF_W
C_A = C_HGRN + 3 * HGRN_QK_W + 2 * HGRN_W
IN_WP = C_A + A_PAD

VMEM_LIMIT = 56 * 1024 * 1024

ROW_TILE = 512
OUT_ROW_TILE = 512
Q_TILE = 256
SCAN_TILE = 128


def _dot(a, b):
    return jnp.dot(a, b, preferred_element_type=F32)


def _dot_nt(a, b):
    return lax.dot_general(a, b, (((1,), (1,)), ((), ())), preferred_element_type=F32)


def _dot_tn(a, b):
    return lax.dot_general(a, b, (((0,), (0,)), ((), ())), preferred_element_type=F32)


def _split_bf16(x):
    hi = x.astype(BF16)
    lo = (x - hi.astype(F32)).astype(BF16)
    return hi, lo


def _sigmoid(x):
    return 1.0 / (1.0 + jnp.exp(-x))


def _silu(x):
    return x * _sigmoid(x)


def _inproj_kernel(x_ref, *refs):
    _inproj_body(x_ref[...], *refs)


def _inproj_body(x, nw_ref, w_ref, wa2_ref, ba_ref, cos_ref, sin_ref,
                 aq_ref, ak_ref, av_ref, alg_ref, dq_ref, dk_ref, dv_ref,
                 hq_ref, hz_ref, hi_ref, g_ref):
    h = x * lax.rsqrt(jnp.mean(x * x, axis=-1, keepdims=True) + EPS) * nw_ref[...]
    hb = h.astype(BF16)

    def proj(c0, width):
        return _dot(hb, w_ref[:, c0:c0 + width])

    p = proj(C_GLA, 2 * GLA_QK_W + 2 * GLA_W)
    aq_ref[...] = (p[:, :GLA_QK_W] * (GLA_DK ** -0.5)).astype(BF16)
    ak_ref[...] = p[:, GLA_QK_W:2 * GLA_QK_W].astype(BF16)
    av_ref[...] = p[:, 2 * GLA_QK_W:2 * GLA_QK_W + GLA_W].astype(BF16)
    g_ref[:, 0:GLA_W] = p[:, 2 * GLA_QK_W + GLA_W:]
    a = proj(C_A, A_PAD)
    zz = _dot(a.astype(BF16), wa2_ref[...]) + ba_ref[...]
    alg_ref[...] = (jnp.minimum(zz, 0.0) - jnp.log1p(jnp.exp(-jnp.abs(zz)))) * (LOG2E / GLA_GATE_TEMP)

    cos = cos_ref[...]
    sin = sin_ref[...]
    lane = lax.broadcasted_iota(jnp.int32, (1, LANES), 1)
    first_half = (lane % DIFF_HD) < (DIFF_HD // 2)

    def rope_store(c0, out_ref, scale):
        pq = proj(c0, DIFF_QK_W)
        for j in range(DIFF_QK_W // LANES):
            xs = pq[:, j * LANES:(j + 1) * LANES]
            partner = jnp.where(first_half,
                                pltpu.roll(xs, LANES - DIFF_HD // 2, 1),
                                pltpu.roll(xs, DIFF_HD // 2, 1))
            r = xs * cos + partner * sin
            if scale != 1.0:
                r = r * scale
            out_ref[:, j * LANES:(j + 1) * LANES] = r.astype(BF16)

    rope_store(C_DIFF, dq_ref, DIFF_HD ** -0.5 * LOG2E)
    rope_store(C_DIFF + DIFF_QK_W, dk_ref, 1.0)
    p = proj(C_DIFF + 2 * DIFF_QK_W, 2 * DIFF_W)
    dv_ref[...] = p[:, :DIFF_W].astype(BF16)
    g_ref[:, GLA_W:GLA_W + DIFF_W] = p[:, DIFF_W:]

    p = proj(C_HGRN, 3 * HGRN_QK_W + 2 * HGRN_W)
    hq_ref[...] = p[:, :HGRN_QK_W].astype(BF16)
    hz_ref[...] = p[:, HGRN_QK_W:3 * HGRN_QK_W]
    hi_ref[...] = p[:, 3 * HGRN_QK_W:3 * HGRN_QK_W + HGRN_W].astype(BF16)
    g_ref[:, GLA_W + DIFF_W:] = p[:, 3 * HGRN_QK_W + HGRN_W:]


def _layer_block(arr, layer):
    return pl.BlockSpec((None,) + arr.shape[1:], lambda i: (layer, 0, 0), pipeline_mode=pl.Buffered(1))


def _inproj(xf, nw, w, wa2, ba, cos_t, sin_t, seq_len, layer, prev=None):
    n = xf.shape[0]
    tm = ROW_TILE
    n_pos_tiles = seq_len // tm
    row = lambda i: (i, 0)
    const = lambda i: (0, 0)
    pos = lambda i: (i % n_pos_tiles, 0)
    widths = [(GLA_QK_W, BF16), (GLA_QK_W, BF16), (GLA_W, BF16), (2 * GLA_QK_W, F32),
              (DIFF_QK_W, BF16), (DIFF_QK_W, BF16), (DIFF_W, BF16),
              (HGRN_QK_W, BF16), (2 * HGRN_QK_W, F32), (HGRN_W, BF16), (MIX_W, F32)]
    in_specs = [pl.BlockSpec((1, D_MODEL), const),
                _layer_block(w, layer),
                _layer_block(wa2, layer),
                pl.BlockSpec((1, 2 * GLA_QK_W), const),
                pl.BlockSpec((tm, LANES), pos),
                pl.BlockSpec((tm, LANES), pos)]
    operands = (nw, w, wa2, ba, cos_t, sin_t)
    if prev is None:
        body, name = _inproj_kernel, "inproj"
        lead_specs, lead = [pl.BlockSpec((tm, D_MODEL), row)], (xf,)
    else:
        body, name = _out_in_kernel, "outproj_inproj"
        widths = [(D_MODEL, F32)] + widths
        lead_specs = [pl.BlockSpec((tm, D_MODEL), row),
                      pl.BlockSpec((tm, GLA_W), row),
                      pl.BlockSpec((tm, DIFF_W), row),
                      pl.BlockSpec((tm, HGRN_W), row),
                      _layer_block(prev[3], layer - 1),
                      pl.BlockSpec((1, D_MODEL), const)]
        lead = (xf,) + tuple(prev)
    return pl.pallas_call(
        body,
        grid=(n // tm,),
        in_specs=lead_specs + in_specs,
        out_specs=[pl.BlockSpec((tm, wd), row) for wd, _ in widths],
        out_shape=[jax.ShapeDtypeStruct((n, wd), dt) for wd, dt in widths],
        compiler_params=pltpu.CompilerParams(dimension_semantics=("parallel",),
                                             vmem_limit_bytes=VMEM_LIMIT),
        name=name,
    )(*lead, *operands)


HEADS_PER_STEP = 2
ONES_ROWS = 16
KEY_CHUNK = 512
SCORE_LOOKAHEAD = 6


def _attn_kernel(q_ref, k_ref, v_ref, g_ref, nw_ref, lq1_ref, lk1_ref, lq2_ref, lk2_ref, o_ref, vt_scr,
                 *, lambda_init):
    seq_len = k_ref.shape[0]
    head_cols = [slice(hd * DIFF_DV, (hd + 1) * DIFF_DV) for hd in range(HEADS_PER_STEP)]
    for hd, cols in enumerate(head_cols):
        vt_scr[hd, 0:DIFF_DV, :] = v_ref[:, cols].astype(F32).T.astype(BF16)
        vt_scr[hd, DIFF_DV:, :] = jnp.ones((ONES_ROWS, seq_len), BF16)

    lam = (jnp.exp(jnp.sum(lq1_ref[...] * lk1_ref[...], axis=-1, keepdims=True))
           - jnp.exp(jnp.sum(lq2_ref[...] * lk2_ref[...], axis=-1, keepdims=True)) + lambda_init)
    lane = lax.broadcasted_iota(jnp.int32, (1, 2 * DIFF_HD), 1)
    nk = seq_len // KEY_CHUNK
    items = [(hd, qi, c, j) for hd in range(HEADS_PER_STEP) for qi in range(seq_len // Q_TILE)
             for c in range(2) for j in range(nk)]
    qmap_cache = {}

    def qmap(hd, qi, c):
        if (hd, qi, c) not in qmap_cache:
            q = q_ref[qi * Q_TILE:(qi + 1) * Q_TILE, head_cols[hd]]
            qmap_cache[(hd, qi, c)] = jnp.where((lane // DIFF_HD) == c, q, jnp.zeros_like(q))
        return qmap_cache[(hd, qi, c)]

    def scores(item):
        hd, qi, c, j = item
        return _dot_nt(k_ref[j * KEY_CHUNK:(j + 1) * KEY_CHUNK, head_cols[hd]], qmap(hd, qi, c))

    ahead = [scores(it) for it in items[:SCORE_LOOKAHEAD]]
    parts, maxes, map_out = [], [], []
    for n, (hd, qi, c, j) in enumerate(items):
        st = ahead.pop(0)
        if n + SCORE_LOOKAHEAD < len(items):
            ahead.append(scores(items[n + SCORE_LOOKAHEAD]))
        m = jnp.max(st, axis=0, keepdims=True)
        e = jnp.exp2(st - m).astype(BF16)
        parts.append(_dot(vt_scr[hd, :, j * KEY_CHUNK:(j + 1) * KEY_CHUNK], e))
        maxes.append(m)
        if j < nk - 1:
            continue
        m_all = functools.reduce(jnp.maximum, maxes)
        tot = parts[0] * jnp.exp2(maxes[0] - m_all)
        for part, mj in zip(parts[1:], maxes[1:]):
            tot = tot + part * jnp.exp2(mj - m_all)
        map_out.append(tot[0:DIFF_DV, :] * (1.0 / tot[DIFF_DV:DIFF_DV + 1, :]))
        parts, maxes = [], []
        if c == 0:
            continue
        rows = slice(qi * Q_TILE, (qi + 1) * Q_TILE)
        o = (map_out[0] - lam * map_out[1]).T
        map_out = []
        o = o * lax.rsqrt(jnp.mean(o * o, axis=-1, keepdims=True) + EPS) * nw_ref[...] * (1.0 - lambda_init)
        o_ref[rows, head_cols[hd]] = (o * _silu(g_ref[rows, head_cols[hd]])).astype(BF16)


def _attention(dq, dk, dv, g, nw, lq1, lk1, lq2, lk2, batch, seq_len, lambda_init):
    n = dq.shape[0]
    wd = HEADS_PER_STEP * DIFF_DV
    assert DIFF_HEADS % HEADS_PER_STEP == 0 and GLA_W % wd == 0 and 2 * DIFF_HD == DIFF_DV
    bh = lambda b, h: (b, h)
    gmap = lambda b, h: (b, GLA_W // wd + h)
    const = lambda b, h: (0, 0)
    small = pl.BlockSpec((1, DIFF_HD), const)
    return pl.pallas_call(
        functools.partial(_attn_kernel, lambda_init=lambda_init),
        grid=(batch, DIFF_HEADS // HEADS_PER_STEP),
        in_specs=[pl.BlockSpec((seq_len, wd), bh),
                  pl.BlockSpec((seq_len, wd), bh),
                  pl.BlockSpec((seq_len, wd), bh),
                  pl.BlockSpec((seq_len, wd), gmap),
                  pl.BlockSpec((1, DIFF_DV), const),
                  small, small, small, small],
        out_specs=pl.BlockSpec((seq_len, wd), bh),
        out_shape=jax.ShapeDtypeStruct((n, DIFF_W), BF16),
        scratch_shapes=[pltpu.VMEM((HEADS_PER_STEP, DIFF_DV + ONES_ROWS, seq_len), BF16)],
        compiler_params=pltpu.CompilerParams(dimension_semantics=("parallel", "parallel"),
                                             vmem_limit_bytes=VMEM_LIMIT),
        name="diff_attn",
    )(dq, dk, dv, g, nw, lq1, lk1, lq2, lk2)


def _scan_levels(tile):
    nlev = int(math.log2(tile))
    assert 1 << nlev == tile
    return [tile >> (j + 1) for j in range(nlev)]


def _tri_matrices(tile):
    idx = np.arange(tile)
    tril = (idx[None, :] <= idx[:, None]).astype(np.float32)
    return jnp.asarray(tril, BF16), jnp.asarray(tril.T, BF16)


def _interleave(generators):
    live = list(generators)
    while live:
        for gen in list(live):
            try:
                next(gen)
            except StopIteration:
                live.remove(gen)


def _run_scans(scans, n_tiles):
    for init, _, _ in scans:
        init()

    def fwd(i, carry):
        _interleave([steps(i) for _, steps, _ in scans])
        return carry

    lax.fori_loop(0, n_tiles, fwd, 0, unroll=2)

    def bwd(n, carry):
        _interleave([steps(n_tiles - 1 - n) for _, _, steps in scans])
        return carry

    lax.fori_loop(0, n_tiles, bwd, 0, unroll=8)


def _make_scan(load_tile, v_ref, g_ref, nw, tril_ref, triu_ref, bd_ref, y_ref,
               pf_scr, sb_scr, o_scr, qb_scr, kb_scr, totb_scr, stf_scr, stb_scr,
               *, tile, heads, dk, dv):
    t_ = tile
    levels = _scan_levels(t_)
    w = heads * dk
    n_vt = heads * dv // LANES
    hpv = LANES // dv
    sub8 = lax.broadcasted_iota(jnp.int32, (8, 1), 0)
    lane = lax.broadcasted_iota(jnp.int32, (1, LANES), 1)
    xr = (lax.broadcasted_iota(jnp.int32, (t_, hpv * t_), 0)
          ^ (lax.broadcasted_iota(jnp.int32, (t_, hpv * t_), 1) & (t_ - 1)))
    vrow = lax.broadcasted_iota(jnp.int32, (LANES, LANES), 0)
    kcol = lax.broadcasted_iota(jnp.int32, (LANES, LANES), 1)

    def tile_info(p):
        first = p * hpv * dk
        cols = slice((first // LANES) * LANES, (first // LANES + 1) * LANES)
        off = first % LANES
        kmasks = [(lane >= off + r * dk) & (lane < off + (r + 1) * dk) for r in range(hpv)]
        vmasks = [(lane >= r * dv) & (lane < (r + 1) * dv) for r in range(hpv)]
        valid = (vrow // dv) == ((kcol - off) // dk)
        valid = valid & (kcol >= off) & (kcol < off + hpv * dk)
        return cols, kmasks, vmasks, valid

    def stack_heads(x, masks):
        return jnp.concatenate([jnp.where(mk, x, jnp.zeros_like(x)) for mk in masks], axis=0)

    def init():
        stf_scr[...] = jnp.zeros_like(stf_scr)
        stb_scr[...] = jnp.zeros_like(stb_scr)

    def fwd_steps(i):
        r0 = pl.multiple_of(i * t_, t_)
        rows = pl.ds(r0, t_)
        q, kf, kb, lgf, lgb = load_tile(rows)
        hi_f, lo_f = _split_bf16(lgf)
        hi_b, lo_b = _split_bf16(lgb)
        pp = _dot(tril_ref[...], jnp.concatenate([hi_f, lo_f], axis=1))
        ss = _dot(triu_ref[...], jnp.concatenate([hi_b, lo_b], axis=1))
        pf = pp[:, :w] + pp[:, w:]
        sb = ss[:, :w] + ss[:, w:]
        pf_scr[...] = pf
        sb_scr[...] = sb
        yield
        row_cache = {}

        def brow(scr, r):
            key = (id(scr), r)
            if key not in row_cache:
                row_cache[key] = jnp.broadcast_to(scr[r:r + 1, :], (8, w))
            return row_cache[key]

        def level_exponents(m):
            tgt, src = [], []
            for j in range(t_ // 8):
                r8 = slice(8 * j, 8 * j + 8)
                pf_t = pf_scr[r8, :]
                sb_t = sb_scr[r8, :]
                if m >= 8:
                    mid = (8 * j // (2 * m)) * 2 * m + m
                    gf, gb = brow(pf_scr, mid - 1), brow(sb_scr, mid)
                    if (8 * j // m) % 2 == 0:
                        tgt.append(sb_t - gb)
                        src.append(gf - pf_t)
                    else:
                        tgt.append(pf_t - gf)
                        src.append(gb - sb_t)
                    continue
                odd = (sub8 & m) != 0
                if m == 1:
                    tgt.append(jnp.where(odd, lgf[r8], lgb[r8]))
                    continue
                if m == 4:
                    gf, gb = brow(pf_scr, 8 * j + 3), brow(sb_scr, 8 * j + 4)
                else:
                    low = sub8 < 4
                    gf = jnp.where(low, brow(pf_scr, 8 * j + 1), brow(pf_scr, 8 * j + 5))
                    gb = jnp.where(low, brow(sb_scr, 8 * j + 2), brow(sb_scr, 8 * j + 6))
                d_f = pf_t - gf
                d_b = sb_t - gb
                tgt.append(jnp.where(odd, d_f, d_b))
                src.append(-jnp.where(odd, d_b, d_f))
            return tgt, src

        def pow2_16(pieces):
            return jnp.exp2(jnp.concatenate(pieces, axis=0)).astype(BF16)

        q16 = q.astype(BF16)
        kf16 = kf.astype(BF16)
        kb16 = kb.astype(BF16)
        row = lax.broadcasted_iota(jnp.int32, (t_, 1), 0)
        info = [tile_info(p) for p in range(n_vt)]
        acc = [None] * n_vt
        for li, m in enumerate(levels):
            tgt, src = level_exponents(m)
            lhs = q16 * pow2_16(tgt)
            if m >= 16:
                kk = jnp.concatenate([(kb16 if (r // m) % 2 else kf16)[r:r + m] for r in range(0, t_, m)], axis=0)
            else:
                kk = jnp.where((row & m) != 0, kb16, kf16)
            rhs = kk if m == 1 else kk * pow2_16(src)
            yield
            for p, (cols, kmasks, _, _) in enumerate(info):
                a = _dot_nt(lhs[:, cols], stack_heads(rhs[:, cols], kmasks))
                acc[p] = a if li == 0 else jnp.where(xr < 2 * m, a, acc[p])
                yield
        ksum16 = (kf + kb).astype(BF16)
        for p, (cols, kmasks, _, _) in enumerate(info):
            acc[p] = jnp.where(xr == 0, _dot_nt(q16[:, cols], stack_heads(ksum16[:, cols], kmasks)), acc[p])
        yield
        totf = pf_scr[t_ - 1:t_, :]
        totb = sb_scr[0:1, :]
        qf = q16 * jnp.exp2(pf).astype(BF16)
        kfd = kf16 * jnp.exp2(totf - pf).astype(BF16)
        yield
        qb_scr[rows, :] = q16 * jnp.exp2(sb).astype(BF16)
        kb_scr[rows, :] = kb16 * jnp.exp2(totb - sb).astype(BF16)
        dec_f = jnp.exp2(totf)
        totb_scr[i] = jnp.broadcast_to(jnp.exp2(totb), (8, w))
        yield

        v = v_ref[rows, :]
        outs = []
        for p, (cols, _, vmasks, valid) in enumerate(info):
            v_p = v[:, p * LANES:(p + 1) * LANES]
            st = stf_scr[p]
            outs.append(_dot(acc[p].astype(BF16), stack_heads(v_p, vmasks))
                        + _dot_nt(qf[:, cols], st.astype(BF16)))
            stf_scr[p] = jnp.where(valid, st * dec_f[:, cols] + _dot_tn(v_p, kfd[:, cols]), 0.0)
            yield
        o_scr[rows, :] = jnp.concatenate(outs, axis=1)

    def bwd_steps(i):
        r0 = pl.multiple_of(i * t_, t_)
        rows = pl.ds(r0, t_)
        v = v_ref[rows, :]
        qb = qb_scr[rows, :]
        kbd = kb_scr[rows, :]
        dec_b = totb_scr[i][0:1, :]
        outs = []
        for p in range(n_vt):
            cols, _, _, valid = tile_info(p)
            st = stb_scr[p]
            outs.append(_dot_nt(qb[:, cols], st.astype(BF16)))
            stb_scr[p] = jnp.where(valid, st * dec_b[:, cols] + _dot_tn(v[:, p * LANES:(p + 1) * LANES],
                                                                       kbd[:, cols]), 0.0)
            yield
        o = o_scr[rows, :] + jnp.concatenate(outs, axis=1)
        hi, lo = _split_bf16(o * o)
        bd = bd_ref[...]
        ms = _dot(hi, bd) + _dot(lo, bd)
        yield
        y = o * lax.rsqrt(ms + EPS) * nw * _silu(g_ref[rows, :])
        y_ref[rows, :] = y.astype(BF16)

    return init, fwd_steps, bwd_steps


N_SCAN_SCRATCH = 8


def _scans_kernel(aq_ref, ak_ref, alg_ref, av_ref, ag_ref, anw_ref, abd_ref,
                  hq_ref, hz_ref, lbl_ref, hv_ref, hg_ref, hnw_ref, hbd_ref, tril_ref, triu_ref,
                  ya_ref, yc_ref, *scratch, seq_len, tile, layer):
    def gla_tile(rows):
        k = ak_ref[rows, :]
        lg = alg_ref[rows, :]
        return aq_ref[rows, :], k, k, lg[:, :GLA_QK_W], lg[:, GLA_QK_W:]

    gla = _make_scan(gla_tile, av_ref, ag_ref, jnp.concatenate([anw_ref[...]] * GLA_HEADS, axis=1),
                     tril_ref, triu_ref, abd_ref, ya_ref, *scratch[:N_SCAN_SCRATCH],
                     tile=tile, heads=GLA_HEADS, dk=GLA_DK, dv=GLA_DV)

    logits = lbl_ref[...]
    e = jnp.exp(logits - jnp.max(logits, axis=0, keepdims=True))
    share = e / jnp.sum(e, axis=0, keepdims=True)
    lb = jnp.zeros((1, HGRN_QK_W), F32)
    for j in range(1, layer + 1):
        lb = lb + share[j:j + 1, :]

    def gates(z):
        t = jnp.exp(-jnp.abs(z))
        r = 1.0 / (1.0 + t)
        pos = z >= 0.0
        sig = jnp.where(pos, r, t * r)
        sig_neg = jnp.where(pos, t * r, r)
        f = lb + (1.0 - lb) * sig
        return jnp.log(jnp.maximum(f, F_FLOOR)) * LOG2E, (1.0 - lb) * sig_neg

    def hgrn_tile(rows):
        z = hz_ref[rows, :]
        lgf, kf = gates(z[:, :HGRN_QK_W])
        lgb, kb = gates(z[:, HGRN_QK_W:])
        return hq_ref[rows, :], kf, kb, lgf, lgb

    hgrn = _make_scan(hgrn_tile, hv_ref, hg_ref, jnp.concatenate([hnw_ref[...]] * HGRN_HEADS, axis=1),
                      tril_ref, triu_ref, hbd_ref, yc_ref, *scratch[N_SCAN_SCRATCH:],
                      tile=tile, heads=HGRN_HEADS, dk=HGRN_DK, dv=HGRN_DV)
    _run_scans([gla, hgrn], seq_len // tile)


def _scan_scratch(seq_len, tile, heads, dk, dv):
    w = heads * dk
    n_vt = heads * dv // LANES
    return [pltpu.VMEM((tile, w), F32),
            pltpu.VMEM((tile, w), F32),
            pltpu.VMEM((seq_len, heads * dv), F32),
            pltpu.VMEM((seq_len, w), BF16),
            pltpu.VMEM((seq_len, w), BF16),
            pltpu.VMEM((seq_len // tile, 8, w), F32),
            pltpu.VMEM((n_vt, LANES, LANES), F32),
            pltpu.VMEM((n_vt, LANES, LANES), F32)]


def _scans(aq, ak, alg, av, hq, hz, hi, g, gla_nw, hgrn_nw, lb_logits, tri, bd_gla, bd_hgrn,
           batch, seq_len, layer):
    n = batch * seq_len
    const = lambda b: (0, 0)
    seq = lambda wd, col=0: pl.BlockSpec((seq_len, wd), lambda b: (b, col))
    whole = lambda arr: pl.BlockSpec(arr.shape, const)
    scratch = (_scan_scratch(seq_len, SCAN_TILE, GLA_HEADS, GLA_DK, GLA_DV)
               + _scan_scratch(seq_len, SCAN_TILE, HGRN_HEADS, HGRN_DK, HGRN_DV))
    assert len(scratch) == 2 * N_SCAN_SCRATCH
    return pl.pallas_call(
        functools.partial(_scans_kernel, seq_len=seq_len, tile=SCAN_TILE, layer=layer),
        grid=(batch,),
        in_specs=[seq(GLA_QK_W), seq(GLA_QK_W), seq(2 * GLA_QK_W), seq(GLA_W), seq(GLA_W, 0),
                  whole(gla_nw), whole(bd_gla),
                  seq(HGRN_QK_W), seq(2 * HGRN_QK_W), whole(lb_logits), seq(HGRN_W),
                  seq(HGRN_W, (GLA_W + DIFF_W) // HGRN_W), whole(hgrn_nw), whole(bd_hgrn),
                  whole(tri[0]), whole(tri[1])],
        out_specs=[seq(GLA_W), seq(HGRN_W)],
        out_shape=[jax.ShapeDtypeStruct((n, GLA_W), BF16), jax.ShapeDtypeStruct((n, HGRN_W), BF16)],
        scratch_shapes=scratch,
        compiler_params=pltpu.CompilerParams(dimension_semantics=("parallel",),
                                             vmem_limit_bytes=VMEM_LIMIT),
        name="scans",
    )(aq, ak, alg, av, g, gla_nw, bd_gla, hq, hz, lb_logits, hi, g, hgrn_nw, bd_hgrn, tri[0], tri[1])


def _outproj_value(x_ref, ya_ref, yb_ref, yc_ref, w_ref, nw_ref):
    y = (_dot(ya_ref[...], w_ref[0:GLA_W, :])
         + _dot(yb_ref[...], w_ref[GLA_W:GLA_W + DIFF_W, :])
         + _dot(yc_ref[...], w_ref[GLA_W + DIFF_W:, :]))
    y = y * lax.rsqrt(jnp.mean(y * y, axis=-1, keepdims=True) + EPS) * nw_ref[...]
    return x_ref[...] + y


def _outproj_kernel(x_ref, ya_ref, yb_ref, yc_ref, w_ref, nw_ref, o_ref):
    o_ref[...] = _outproj_value(x_ref, ya_ref, yb_ref, yc_ref, w_ref, nw_ref)


def _out_in_kernel(x_ref, ya_ref, yb_ref, yc_ref, wout_ref, npost_ref,
                   npre_ref, w_ref, wa2_ref, ba_ref, cos_ref, sin_ref, xnew_ref, *outs):
    x = _outproj_value(x_ref, ya_ref, yb_ref, yc_ref, wout_ref, npost_ref)
    xnew_ref[...] = x
    _inproj_body(x, npre_ref, w_ref, wa2_ref, ba_ref, cos_ref, sin_ref, *outs)


def _outproj(xf, ya, yb, yc, w, nw, layer):
    n = xf.shape[0]
    tm = OUT_ROW_TILE
    row = lambda i: (i, 0)
    const = lambda i: (0, 0)
    return pl.pallas_call(
        _outproj_kernel,
        grid=(n // tm,),
        in_specs=[pl.BlockSpec((tm, D_MODEL), row),
                  pl.BlockSpec((tm, GLA_W), row),
                  pl.BlockSpec((tm, DIFF_W), row),
                  pl.BlockSpec((tm, HGRN_W), row),
                  _layer_block(w, layer),
                  pl.BlockSpec((1, D_MODEL), const)],
        out_specs=pl.BlockSpec((tm, D_MODEL), row),
        out_shape=jax.ShapeDtypeStruct((n, D_MODEL), F32),
        compiler_params=pltpu.CompilerParams(dimension_semantics=("parallel",),
                                             vmem_limit_bytes=VMEM_LIMIT),
        name="outproj",
    )(xf, ya, yb, yc, w, nw)


def _block_mean_matrix(heads, dv):
    m = np.kron(np.eye(heads, dtype=np.float32), np.full((dv, dv), 1.0 / dv, np.float32))
    return jnp.asarray(m, BF16)


def kernel(x, norm_pre, norm_post, w_in, w_out, gla_wa2_fwd, gla_ba_fwd, gla_wa2_bwd, gla_ba_bwd, gla_norm,
           diff_lq1, diff_lk1, diff_lq2, diff_lk2, diff_norm, hgrn_lb_logits, hgrn_norm):
    batch, seq_len, d_model = x.shape
    assert d_model == D_MODEL and w_in.shape == (DEPTH, D_MODEL, IN_W)
    assert seq_len % ROW_TILE == 0 and seq_len % Q_TILE == 0 and seq_len % SCAN_TILE == 0
    assert seq_len % KEY_CHUNK == 0 and (batch * seq_len) % OUT_ROW_TILE == 0
    n = batch * seq_len
    xf = x.reshape(n, D_MODEL)

    a0 = 2 * GLA_QK_W + 2 * GLA_W
    w_in_b = w_in.astype(BF16)
    w_perm = jnp.concatenate(
        [w_in_b[:, :, :a0], w_in_b[:, :, a0 + 2 * GLA_RANK:], w_in_b[:, :, a0:a0 + 2 * GLA_RANK],
         jnp.zeros((DEPTH, D_MODEL, A_PAD - 2 * GLA_RANK), BF16)], axis=-1)
    wa2 = jnp.zeros((DEPTH, A_PAD, 2 * GLA_QK_W), F32)
    wa2 = wa2.at[:, :GLA_RANK, :GLA_QK_W].set(gla_wa2_fwd)
    wa2 = wa2.at[:, GLA_RANK:2 * GLA_RANK, GLA_QK_W:].set(gla_wa2_bwd).astype(BF16)
    ba = jnp.concatenate([gla_ba_fwd, gla_ba_bwd], axis=-1)
    w_out_b = w_out.astype(BF16)

    inv_freq = ROPE_THETA ** (-jnp.arange(0, DIFF_HD, 2, dtype=F32) / DIFF_HD)
    ang = jnp.arange(seq_len, dtype=jnp.int32).astype(F32)[:, None] * inv_freq[None, :]
    cos_t = jnp.tile(jnp.cos(ang), (1, 2 * LANES // DIFF_HD))
    sin_t = jnp.tile(jnp.concatenate([-jnp.sin(ang), jnp.sin(ang)], axis=-1), (1, LANES // DIFF_HD))

    tri = _tri_matrices(SCAN_TILE)
    bd_gla = _block_mean_matrix(GLA_HEADS, GLA_DV)
    bd_hgrn = _block_mean_matrix(HGRN_HEADS, HGRN_DV)

    prev = None
    for layer in range(DEPTH):
        lambda_init = 0.8 - 0.6 * math.exp(-0.3 * layer)
        outs = _inproj(xf, norm_pre[layer][None, :], w_perm, wa2, ba[layer][None, :],
                       cos_t, sin_t, seq_len, layer, prev)
        if prev is not None:
            xf, outs = outs[0], outs[1:]
        (aq, ak, av, alg, dq, dk, dv, hq, hz, hi, g) = outs
        yb = _attention(dq, dk, dv, g, diff_norm[layer][None, :], diff_lq1[layer][None, :],
                        diff_lk1[layer][None, :], diff_lq2[layer][None, :], diff_lk2[layer][None, :],
                        batch, seq_len, lambda_init)
        ya, yc = _scans(aq, ak, alg, av, hq, hz, hi, g, gla_norm[layer][None, :], hgrn_norm[layer][None, :],
                        hgrn_lb_logits, tri, bd_gla, bd_hgrn, batch, seq_len, layer)
        prev = (ya, yb, yc, w_out_b, norm_post[layer][None, :])
    xf = _outproj(xf, *prev, DEPTH - 1)
    return xf.reshape(batch, seq_len, D_MODEL)
```

```python
import functools
import math

import numpy as np
import jax
import jax.numpy as jnp
from jax import lax
from jax.experimental import pallas as pl
from jax.experimental.pallas import tpu as pltpu

F32 = jnp.float32
BF16 = jnp.bfloat16

D_MODEL = 1024
DEPTH = 4
GLA_HEADS, GLA_DK, GLA_DV, GLA_RANK = 4, 32, 64, 16
GLA_GATE_TEMP = 16.0
DIFF_HEADS, DIFF_HD = 4, 64
DIFF_DV = 2 * DIFF_HD
HGRN_HEADS, HGRN_DK, HGRN_DV = 4, 64, 64
ROPE_THETA = 10000.0
EPS = 1e-6
F_FLOOR = 1e-30
LOG2E = math.log2(math.e)

GLA_QK_W = GLA_HEADS * GLA_DK
GLA_W = GLA_HEADS * GLA_DV
DIFF_QK_W = DIFF_HEADS * 2 * DIFF_HD
DIFF_W = DIFF_HEADS * DIFF_DV
HGRN_QK_W = HGRN_HEADS * HGRN_DK
HGRN_W = HGRN_HEADS * HGRN_DV
MIX_W = GLA_W + DIFF_W + HGRN_W
IN_W = 2 * GLA_QK_W + 2 * GLA_W + 2 * GLA_RANK + 2 * DIFF_QK_W + 2 * DIFF_W + 3 * HGRN_QK_W + 2 * HGRN_W

LANES = 128
A_PAD = LANES
C_GLA = 0
C_DIFF = C_GLA + 2 * GLA_QK_W + 2 * GLA_W
C_HGRN = C_DIFF + 2 * DIFF_QK_W + 2 * DIFF_W
C_A = C_HGRN + 3 * HGRN_QK_W + 2 * HGRN_W
IN_WP = C_A + A_PAD

VMEM_LIMIT = 56 * 1024 * 1024

ROW_TILE = 512
OUT_ROW_TILE = 512
Q_TILE = 256
SCAN_TILE = 128


def _dot(a, b):
    return jnp.dot(a, b, preferred_element_type=F32)


def _dot_nt(a, b):
    return lax.dot_general(a, b, (((1,), (1,)), ((), ())), preferred_element_type=F32)


def _dot_tn(a, b):
    return lax.dot_general(a, b, (((0,), (0,)), ((), ())), preferred_element_type=F32)


def _split_bf16(x):
    hi = x.astype(BF16)
    lo = (x - hi.astype(F32)).astype(BF16)
    return hi, lo


def _sigmoid(x):
    return 1.0 / (1.0 + jnp.exp(-x))


def _silu(x):
    return x * _sigmoid(x)


def _inproj_kernel(x_ref, *refs):
    _inproj_body(x_ref[...], *refs)


def _inproj_body(x, nw_ref, w_raw_ref, wa2_ref, ba_ref, cos_ref, sin_ref,
                 aq_ref, ak_ref, av_ref, alg_ref, dq_ref, dk_ref, dv_ref,
                 hq_ref, hz_ref, hi_ref, g_ref, w_ref):
    @pl.when(pl.program_id(0) == 0)
    def _():
        a0 = 2 * GLA_QK_W + 2 * GLA_W
        w_ref[:, 0:a0] = w_raw_ref[:, 0:a0]
        w_ref[:, a0:C_A] = w_raw_ref[:, a0 + 2 * GLA_RANK:IN_W]
        w_ref[:, C_A:] = jnp.concatenate(
            [w_raw_ref[:, a0:a0 + 2 * GLA_RANK], jnp.zeros((D_MODEL, A_PAD - 2 * GLA_RANK), BF16)], axis=1)

    h = x * lax.rsqrt(jnp.mean(x * x, axis=-1, keepdims=True) + EPS) * nw_ref[...]
    hb = h.astype(BF16)

    def proj(c0, width):
        return _dot(hb, w_ref[:, c0:c0 + width])

    p = proj(C_GLA, 2 * GLA_QK_W + 2 * GLA_W)
    aq_ref[...] = (p[:, :GLA_QK_W] * (GLA_DK ** -0.5)).astype(BF16)
    ak_ref[...] = p[:, GLA_QK_W:2 * GLA_QK_W].astype(BF16)
    av_ref[...] = p[:, 2 * GLA_QK_W:2 * GLA_QK_W + GLA_W].astype(BF16)
    g_ref[:, 0:GLA_W] = p[:, 2 * GLA_QK_W + GLA_W:]
    a = proj(C_A, A_PAD)
    zz = _dot(a.astype(BF16), wa2_ref[...]) + ba_ref[...]
    alg_ref[...] = (jnp.minimum(zz, 0.0) - jnp.log1p(jnp.exp(-jnp.abs(zz)))) * (LOG2E / GLA_GATE_TEMP)

    cos = cos_ref[...]
    sin = sin_ref[...]
    lane = lax.broadcasted_iota(jnp.int32, (1, LANES), 1)
    first_half = (lane % DIFF_HD) < (DIFF_HD // 2)

    def rope_store(c0, out_ref, scale):
        pq = proj(c0, DIFF_QK_W)
        for j in range(DIFF_QK_W // LANES):
            xs = pq[:, j * LANES:(j + 1) * LANES]
            partner = jnp.where(first_half,
                                pltpu.roll(xs, LANES - DIFF_HD // 2, 1),
                                pltpu.roll(xs, DIFF_HD // 2, 1))
            r = xs * cos + partner * sin
            if scale != 1.0:
                r = r * scale
            out_ref[:, j * LANES:(j + 1) * LANES] = r.astype(BF16)

    rope_store(C_DIFF, dq_ref, DIFF_HD ** -0.5 * LOG2E)
    rope_store(C_DIFF + DIFF_QK_W, dk_ref, 1.0)
    p = proj(C_DIFF + 2 * DIFF_QK_W, 2 * DIFF_W)
    dv_ref[...] = p[:, :DIFF_W].astype(BF16)
    g_ref[:, GLA_W:GLA_W + DIFF_W] = p[:, DIFF_W:]

    p = proj(C_HGRN, 3 * HGRN_QK_W + 2 * HGRN_W)
    hq_ref[...] = p[:, :HGRN_QK_W].astype(BF16)
    hz_ref[...] = p[:, HGRN_QK_W:3 * HGRN_QK_W]
    hi_ref[...] = p[:, 3 * HGRN_QK_W:3 * HGRN_QK_W + HGRN_W].astype(BF16)
    g_ref[:, GLA_W + DIFF_W:] = p[:, 3 * HGRN_QK_W + HGRN_W:]


def _layer_block(arr, layer):
    return pl.BlockSpec((None,) + arr.shape[1:], lambda i: (layer, 0, 0), pipeline_mode=pl.Buffered(1))


def _inproj(xf, nw, w, wa2, ba, cos_t, sin_t, seq_len, layer, prev=None):
    n = xf.shape[0]
    tm = ROW_TILE
    n_pos_tiles = seq_len // tm
    row = lambda i: (i, 0)
    const = lambda i: (0, 0)
    pos = lambda i: (i % n_pos_tiles, 0)
    widths = [(GLA_QK_W, BF16), (GLA_QK_W, BF16), (GLA_W, BF16), (2 * GLA_QK_W, F32),
              (DIFF_QK_W, BF16), (DIFF_QK_W, BF16), (DIFF_W, BF16),
              (HGRN_QK_W, BF16), (2 * HGRN_QK_W, F32), (HGRN_W, BF16), (MIX_W, F32)]
    in_specs = [pl.BlockSpec((1, D_MODEL), const),
                _layer_block(w, layer),
                _layer_block(wa2, layer),
                pl.BlockSpec((1, 2 * GLA_QK_W), const),
                pl.BlockSpec((tm, LANES), pos),
                pl.BlockSpec((tm, LANES), pos)]
    operands = (nw, w, wa2, ba, cos_t, sin_t)
    if prev is None:
        body, name = _inproj_kernel, "inproj"
        lead_specs, lead = [pl.BlockSpec((tm, D_MODEL), row)], (xf,)
    else:
        body, name = _out_in_kernel, "outproj_inproj"
        widths = [(D_MODEL, F32)] + widths
        lead_specs = [pl.BlockSpec((tm, D_MODEL), row),
                      pl.BlockSpec((tm, GLA_W), row),
                      pl.BlockSpec((tm, DIFF_W), row),
                      pl.BlockSpec((tm, HGRN_W), row),
                      _layer_block(prev[3], layer - 1),
                      pl.BlockSpec((1, D_MODEL), const)]
        lead = (xf,) + tuple(prev)
    return pl.pallas_call(
        body,
        grid=(n // tm,),
        in_specs=lead_specs + in_specs,
        out_specs=[pl.BlockSpec((tm, wd), row) for wd, _ in widths],
        out_shape=[jax.ShapeDtypeStruct((n, wd), dt) for wd, dt in widths],
        scratch_shapes=[pltpu.VMEM((D_MODEL, IN_WP), BF16)],
        compiler_params=pltpu.CompilerParams(dimension_semantics=("arbitrary",),
                                             vmem_limit_bytes=VMEM_LIMIT),
        name=name,
    )(*lead, *operands)


HEADS_PER_STEP = 2
ONES_ROWS = 16
KEY_CHUNK = 512
SCORE_LOOKAHEAD = 6


def _attn_kernel(q_ref, k_ref, v_ref, g_ref, nw_ref, lq1_ref, lk1_ref, lq2_ref, lk2_ref, o_ref, vt_scr,
                 *, lambda_init):
    seq_len = k_ref.shape[0]
    head_cols = [slice(hd * DIFF_DV, (hd + 1) * DIFF_DV) for hd in range(HEADS_PER_STEP)]
    for hd, cols in enumerate(head_cols):
        vt_scr[hd, 0:DIFF_DV, :] = v_ref[:, cols].astype(F32).T.astype(BF16)
        vt_scr[hd, DIFF_DV:, :] = jnp.ones((ONES_ROWS, seq_len), BF16)

    lam = (jnp.exp(jnp.sum(lq1_ref[...] * lk1_ref[...], axis=-1, keepdims=True))
           - jnp.exp(jnp.sum(lq2_ref[...] * lk2_ref[...], axis=-1, keepdims=True)) + lambda_init)
    lane = lax.broadcasted_iota(jnp.int32, (1, 2 * DIFF_HD), 1)
    nk = seq_len // KEY_CHUNK
    items = [(hd, qi, c, j) for hd in range(HEADS_PER_STEP) for qi in range(seq_len // Q_TILE)
             for c in range(2) for j in range(nk)]
    qmap_cache = {}

    def qmap(hd, qi, c):
        if (hd, qi, c) not in qmap_cache:
            q = q_ref[qi * Q_TILE:(qi + 1) * Q_TILE, head_cols[hd]]
            qmap_cache[(hd, qi, c)] = jnp.where((lane // DIFF_HD) == c, q, jnp.zeros_like(q))
        return qmap_cache[(hd, qi, c)]

    def scores(item):
        hd, qi, c, j = item
        return _dot_nt(k_ref[j * KEY_CHUNK:(j + 1) * KEY_CHUNK, head_cols[hd]], qmap(hd, qi, c))

    ahead = [scores(it) for it in items[:SCORE_LOOKAHEAD]]
    parts, maxes, map_out = [], [], []
    for n, (hd, qi, c, j) in enumerate(items):
        st = ahead.pop(0)
        if n + SCORE_LOOKAHEAD < len(items):
            ahead.append(scores(items[n + SCORE_LOOKAHEAD]))
        m = jnp.max(st, axis=0, keepdims=True)
        e = jnp.exp2(st - m).astype(BF16)
        parts.append(_dot(vt_scr[hd, :, j * KEY_CHUNK:(j + 1) * KEY_CHUNK], e))
        maxes.append(m)
        if j < nk - 1:
            continue
        m_all = functools.reduce(jnp.maximum, maxes)
        tot = parts[0] * jnp.exp2(maxes[0] - m_all)
        for part, mj in zip(parts[1:], maxes[1:]):
            tot = tot + part * jnp.exp2(mj - m_all)
        map_out.append(tot[0:DIFF_DV, :] * (1.0 / tot[DIFF_DV:DIFF_DV + 1, :]))
        parts, maxes = [], []
        if c == 0:
            continue
        rows = slice(qi * Q_TILE, (qi + 1) * Q_TILE)
        o = (map_out[0] - lam * map_out[1]).T
        map_out = []
        o = o * lax.rsqrt(jnp.mean(o * o, axis=-1, keepdims=True) + EPS) * nw_ref[...] * (1.0 - lambda_init)
        o_ref[rows, head_cols[hd]] = (o * _silu(g_ref[rows, head_cols[hd]])).astype(BF16)


def _attention(dq, dk, dv, g, nw, lq1, lk1, lq2, lk2, batch, seq_len, lambda_init):
    n = dq.shape[0]
    wd = HEADS_PER_STEP * DIFF_DV
    assert DIFF_HEADS % HEADS_PER_STEP == 0 and GLA_W % wd == 0 and 2 * DIFF_HD == DIFF_DV
    bh = lambda b, h: (b, h)
    gmap = lambda b, h: (b, GLA_W // wd + h)
    const = lambda b, h: (0, 0)
    small = pl.BlockSpec((1, DIFF_HD), const)
    return pl.pallas_call(
        functools.partial(_attn_kernel, lambda_init=lambda_init),
        grid=(batch, DIFF_HEADS // HEADS_PER_STEP),
        in_specs=[pl.BlockSpec((seq_len, wd), bh),
                  pl.BlockSpec((seq_len, wd), bh),
                  pl.BlockSpec((seq_len, wd), bh),
                  pl.BlockSpec((seq_len, wd), gmap),
                  pl.BlockSpec((1, DIFF_DV), const),
                  small, small, small, small],
        out_specs=pl.BlockSpec((seq_len, wd), bh),
        out_shape=jax.ShapeDtypeStruct((n, DIFF_W), BF16),
        scratch_shapes=[pltpu.VMEM((HEADS_PER_STEP, DIFF_DV + ONES_ROWS, seq_len), BF16)],
        compiler_params=pltpu.CompilerParams(dimension_semantics=("parallel", "parallel"),
                                             vmem_limit_bytes=VMEM_LIMIT),
        name="diff_attn",
    )(dq, dk, dv, g, nw, lq1, lk1, lq2, lk2)


def _scan_levels(tile):
    nlev = int(math.log2(tile))
    assert 1 << nlev == tile
    return [tile >> (j + 1) for j in range(nlev)]


def _tri_matrices(tile):
    idx = np.arange(tile)
    tril = (idx[None, :] <= idx[:, None]).astype(np.float32)
    return jnp.asarray(tril, BF16), jnp.asarray(tril.T, BF16)


def _interleave(generators):
    live = list(generators)
    while live:
        for gen in list(live):
            try:
                next(gen)
            except StopIteration:
                live.remove(gen)


def _run_scans(scans, n_tiles):
    for init, _, _ in scans:
        init()

    def fwd(i, carry):
        _interleave([steps(i) for _, steps, _ in scans])
        return carry

    lax.fori_loop(0, n_tiles, fwd, 0, unroll=2)

    def bwd(n, carry):
        _interleave([steps(n_tiles - 1 - n) for _, _, steps in scans])
        return carry

    lax.fori_loop(0, n_tiles, bwd, 0, unroll=8)


def _make_scan(load_tile, v_ref, g_ref, nw, tril_ref, triu_ref, bd_ref, y_ref,
               pf_scr, sb_scr, o_scr, qb_scr, kb_scr, totb_scr, stf_scr, stb_scr,
               *, tile, heads, dk, dv):
    t_ = tile
    levels = _scan_levels(t_)
    w = heads * dk
    n_vt = heads * dv // LANES
    hpv = LANES // dv
    sub8 = lax.broadcasted_iota(jnp.int32, (8, 1), 0)
    lane = lax.broadcasted_iota(jnp.int32, (1, LANES), 1)
    xr = (lax.broadcasted_iota(jnp.int32, (t_, hpv * t_), 0)
          ^ (lax.broadcasted_iota(jnp.int32, (t_, hpv * t_), 1) & (t_ - 1)))
    vrow = lax.broadcasted_iota(jnp.int32, (LANES, LANES), 0)
    kcol = lax.broadcasted_iota(jnp.int32, (LANES, LANES), 1)

    def tile_info(p):
        first = p * hpv * dk
        cols = slice((first // LANES) * LANES, (first // LANES + 1) * LANES)
        off = first % LANES
        kmasks = [(lane >= off + r * dk) & (lane < off + (r + 1) * dk) for r in range(hpv)]
        vmasks = [(lane >= r * dv) & (lane < (r + 1) * dv) for r in range(hpv)]
        valid = (vrow // dv) == ((kcol - off) // dk)
        valid = valid & (kcol >= off) & (kcol < off + hpv * dk)
        return cols, kmasks, vmasks, valid

    def stack_heads(x, masks):
        return jnp.concatenate([jnp.where(mk, x, jnp.zeros_like(x)) for mk in masks], axis=0)

    def init():
        stf_scr[...] = jnp.zeros_like(stf_scr)
        stb_scr[...] = jnp.zeros_like(stb_scr)

    def fwd_steps(i):
        r0 = pl.multiple_of(i * t_, t_)
        rows = pl.ds(r0, t_)
        q, kf, kb, lgf, lgb = load_tile(rows)
        hi_f, lo_f = _split_bf16(lgf)
        hi_b, lo_b = _split_bf16(lgb)
        pp = _dot(tril_ref[...], jnp.concatenate([hi_f, lo_f], axis=1))
        ss = _dot(triu_ref[...], jnp.concatenate([hi_b, lo_b], axis=1))
        pf = pp[:, :w] + pp[:, w:]
        sb = ss[:, :w] + ss[:, w:]
        pf_scr[...] = pf
        sb_scr[...] = sb
        yield
        row_cache = {}

        def brow(scr, r):
            key = (id(scr), r)
            if key not in row_cache:
                row_cache[key] = jnp.broadcast_to(scr[r:r + 1, :], (8, w))
            return row_cache[key]

        def level_exponents(m):
            tgt, src = [], []
            for j in range(t_ // 8):
                r8 = slice(8 * j, 8 * j + 8)
                pf_t = pf_scr[r8, :]
                sb_t = sb_scr[r8, :]
                if m >= 8:
                    mid = (8 * j // (2 * m)) * 2 * m + m
                    gf, gb = brow(pf_scr, mid - 1), brow(sb_scr, mid)
                    if (8 * j // m) % 2 == 0:
                        tgt.append(sb_t - gb)
                        src.append(gf - pf_t)
                    else:
                        tgt.append(pf_t - gf)
                        src.append(gb - sb_t)
                    continue
                odd = (sub8 & m) != 0
                if m == 1:
                    tgt.append(jnp.where(odd, lgf[r8], lgb[r8]))
                    continue
                if m == 4:
                    gf, gb = brow(pf_scr, 8 * j + 3), brow(sb_scr, 8 * j + 4)
                else:
                    low = sub8 < 4
                    gf = jnp.where(low, brow(pf_scr, 8 * j + 1), brow(pf_scr, 8 * j + 5))
                    gb = jnp.where(low, brow(sb_scr, 8 * j + 2), brow(sb_scr, 8 * j + 6))
                d_f = pf_t - gf
                d_b = sb_t - gb
                tgt.append(jnp.where(odd, d_f, d_b))
                src.append(-jnp.where(odd, d_b, d_f))
            return tgt, src

        def pow2_16(pieces):
            return jnp.exp2(jnp.concatenate(pieces, axis=0)).astype(BF16)

        q16 = q.astype(BF16)
        kf16 = kf.astype(BF16)
        kb16 = kb.astype(BF16)
        row = lax.broadcasted_iota(jnp.int32, (t_, 1), 0)
        info = [tile_info(p) for p in range(n_vt)]
        acc = [None] * n_vt
        for li, m in enumerate(levels):
            tgt, src = level_exponents(m)
            lhs = q16 * pow2_16(tgt)
            if m >= 16:
                kk = jnp.concatenate([(kb16 if (r // m) % 2 else kf16)[r:r + m] for r in range(0, t_, m)], axis=0)
            else:
                kk = jnp.where((row & m) != 0, kb16, kf16)
            rhs = kk if m == 1 else kk * pow2_16(src)
            yield
            for p, (cols, kmasks, _, _) in enumerate(info):
                a = _dot_nt(lhs[:, cols], stack_heads(rhs[:, cols], kmasks))
                acc[p] = a if li == 0 else jnp.where(xr < 2 * m, a, acc[p])
                yield
        ksum16 = (kf + kb).astype(BF16)
        for p, (cols, kmasks, _, _) in enumerate(info):
            acc[p] = jnp.where(xr == 0, _dot_nt(q16[:, cols], stack_heads(ksum16[:, cols], kmasks)), acc[p])
        yield
        totf = pf_scr[t_ - 1:t_, :]
        totb = sb_scr[0:1, :]
        qf = q16 * jnp.exp2(pf).astype(BF16)
        kfd = kf16 * jnp.exp2(totf - pf).astype(BF16)
        yield
        qb_scr[rows, :] = q16 * jnp.exp2(sb).astype(BF16)
        kb_scr[rows, :] = kb16 * jnp.exp2(totb - sb).astype(BF16)
        dec_f = jnp.exp2(totf)
        totb_scr[i] = jnp.broadcast_to(jnp.exp2(totb), (8, w))
        yield

        v = v_ref[rows, :]
        outs = []
        for p, (cols, _, vmasks, valid) in enumerate(info):
            v_p = v[:, p * LANES:(p + 1) * LANES]
            st = stf_scr[p]
            outs.append(_dot(acc[p].astype(BF16), stack_heads(v_p, vmasks))
                        + _dot_nt(qf[:, cols], st.astype(BF16)))
            stf_scr[p] = jnp.where(valid, st * dec_f[:, cols] + _dot_tn(v_p, kfd[:, cols]), 0.0)
            yield
        o_scr[rows, :] = jnp.concatenate(outs, axis=1)

    def bwd_steps(i):
        r0 = pl.multiple_of(i * t_, t_)
        rows = pl.ds(r0, t_)
        v = v_ref[rows, :]
        qb = qb_scr[rows, :]
        kbd = kb_scr[rows, :]
        dec_b = totb_scr[i][0:1, :]
        outs = []
        for p in range(n_vt):
            cols, _, _, valid = tile_info(p)
            st = stb_scr[p]
            outs.append(_dot_nt(qb[:, cols], st.astype(BF16)))
            stb_scr[p] = jnp.where(valid, st * dec_b[:, cols] + _dot_tn(v[:, p * LANES:(p + 1) * LANES],
                                                                       kbd[:, cols]), 0.0)
            yield
        o = o_scr[rows, :] + jnp.concatenate(outs, axis=1)
        hi, lo = _split_bf16(o * o)
        bd = bd_ref[...]
        ms = _dot(hi, bd) + _dot(lo, bd)
        yield
        y = o * lax.rsqrt(ms + EPS) * nw * _silu(g_ref[rows, :])
        y_ref[rows, :] = y.astype(BF16)

    return init, fwd_steps, bwd_steps


N_SCAN_SCRATCH = 8


def _scans_kernel(aq_ref, ak_ref, alg_ref, av_ref, ag_ref, anw_ref, abd_ref,
                  hq_ref, hz_ref, lbl_ref, hv_ref, hg_ref, hnw_ref, hbd_ref, tril_ref, triu_ref,
                  ya_ref, yc_ref, *scratch, seq_len, tile, layer):
    def gla_tile(rows):
        k = ak_ref[rows, :]
        lg = alg_ref[rows, :]
        return aq_ref[rows, :], k, k, lg[:, :GLA_QK_W], lg[:, GLA_QK_W:]

    gla = _make_scan(gla_tile, av_ref, ag_ref, jnp.concatenate([anw_ref[...]] * GLA_HEADS, axis=1),
                     tril_ref, triu_ref, abd_ref, ya_ref, *scratch[:N_SCAN_SCRATCH],
                     tile=tile, heads=GLA_HEADS, dk=GLA_DK, dv=GLA_DV)

    logits = lbl_ref[...]
    e = jnp.exp(logits - jnp.max(logits, axis=0, keepdims=True))
    share = e / jnp.sum(e, axis=0, keepdims=True)
    lb = jnp.zeros((1, HGRN_QK_W), F32)
    for j in range(1, layer + 1):
        lb = lb + share[j:j + 1, :]

    def gates(z):
        t = jnp.exp(-jnp.abs(z))
        r = 1.0 / (1.0 + t)
        pos = z >= 0.0
        sig = jnp.where(pos, r, t * r)
        sig_neg = jnp.where(pos, t * r, r)
        f = lb + (1.0 - lb) * sig
        return jnp.log(jnp.maximum(f, F_FLOOR)) * LOG2E, (1.0 - lb) * sig_neg

    def hgrn_tile(rows):
        z = hz_ref[rows, :]
        lgf, kf = gates(z[:, :HGRN_QK_W])
        lgb, kb = gates(z[:, HGRN_QK_W:])
        return hq_ref[rows, :], kf, kb, lgf, lgb

    hgrn = _make_scan(hgrn_tile, hv_ref, hg_ref, jnp.concatenate([hnw_ref[...]] * HGRN_HEADS, axis=1),
                      tril_ref, triu_ref, hbd_ref, yc_ref, *scratch[N_SCAN_SCRATCH:],
                      tile=tile, heads=HGRN_HEADS, dk=HGRN_DK, dv=HGRN_DV)
    _run_scans([gla, hgrn], seq_len // tile)


def _scan_scratch(seq_len, tile, heads, dk, dv):
    w = heads * dk
    n_vt = heads * dv // LANES
    return [pltpu.VMEM((tile, w), F32),
            pltpu.VMEM((tile, w), F32),
            pltpu.VMEM((seq_len, heads * dv), F32),
            pltpu.VMEM((seq_len, w), BF16),
            pltpu.VMEM((seq_len, w), BF16),
            pltpu.VMEM((seq_len // tile, 8, w), F32),
            pltpu.VMEM((n_vt, LANES, LANES), F32),
            pltpu.VMEM((n_vt, LANES, LANES), F32)]


def _scans(aq, ak, alg, av, hq, hz, hi, g, gla_nw, hgrn_nw, lb_logits, tri, bd_gla, bd_hgrn,
           batch, seq_len, layer):
    n = batch * seq_len
    const = lambda b: (0, 0)
    seq = lambda wd, col=0: pl.BlockSpec((seq_len, wd), lambda b: (b, col))
    whole = lambda arr: pl.BlockSpec(arr.shape, const)
    scratch = (_scan_scratch(seq_len, SCAN_TILE, GLA_HEADS, GLA_DK, GLA_DV)
               + _scan_scratch(seq_len, SCAN_TILE, HGRN_HEADS, HGRN_DK, HGRN_DV))
    assert len(scratch) == 2 * N_SCAN_SCRATCH
    return pl.pallas_call(
        functools.partial(_scans_kernel, seq_len=seq_len, tile=SCAN_TILE, layer=layer),
        grid=(batch,),
        in_specs=[seq(GLA_QK_W), seq(GLA_QK_W), seq(2 * GLA_QK_W), seq(GLA_W), seq(GLA_W, 0),
                  whole(gla_nw), whole(bd_gla),
                  seq(HGRN_QK_W), seq(2 * HGRN_QK_W), whole(lb_logits), seq(HGRN_W),
                  seq(HGRN_W, (GLA_W + DIFF_W) // HGRN_W), whole(hgrn_nw), whole(bd_hgrn),
                  whole(tri[0]), whole(tri[1])],
        out_specs=[seq(GLA_W), seq(HGRN_W)],
        out_shape=[jax.ShapeDtypeStruct((n, GLA_W), BF16), jax.ShapeDtypeStruct((n, HGRN_W), BF16)],
        scratch_shapes=scratch,
        compiler_params=pltpu.CompilerParams(dimension_semantics=("parallel",),
                                             vmem_limit_bytes=VMEM_LIMIT),
        name="scans",
    )(aq, ak, alg, av, g, gla_nw, bd_gla, hq, hz, lb_logits, hi, g, hgrn_nw, bd_hgrn, tri[0], tri[1])


def _outproj_value(x_ref, ya_ref, yb_ref, yc_ref, w_ref, nw_ref):
    y = (_dot(ya_ref[...], w_ref[0:GLA_W, :])
         + _dot(yb_ref[...], w_ref[GLA_W:GLA_W + DIFF_W, :])
         + _dot(yc_ref[...], w_ref[GLA_W + DIFF_W:, :]))
    y = y * lax.rsqrt(jnp.mean(y * y, axis=-1, keepdims=True) + EPS) * nw_ref[...]
    return x_ref[...] + y


def _outproj_kernel(x_ref, ya_ref, yb_ref, yc_ref, w_ref, nw_ref, o_ref):
    o_ref[...] = _outproj_value(x_ref, ya_ref, yb_ref, yc_ref, w_ref, nw_ref)


def _out_in_kernel(x_ref, ya_ref, yb_ref, yc_ref, wout_ref, npost_ref,
                   npre_ref, w_ref, wa2_ref, ba_ref, cos_ref, sin_ref, xnew_ref, *outs):
    x = _outproj_value(x_ref, ya_ref, yb_ref, yc_ref, wout_ref, npost_ref)
    xnew_ref[...] = x
    _inproj_body(x, npre_ref, w_ref, wa2_ref, ba_ref, cos_ref, sin_ref, *outs)


def _outproj(xf, ya, yb, yc, w, nw, layer):
    n = xf.shape[0]
    tm = OUT_ROW_TILE
    row = lambda i: (i, 0)
    const = lambda i: (0, 0)
    return pl.pallas_call(
        _outproj_kernel,
        grid=(n // tm,),
        in_specs=[pl.BlockSpec((tm, D_MODEL), row),
                  pl.BlockSpec((tm, GLA_W), row),
                  pl.BlockSpec((tm, DIFF_W), row),
                  pl.BlockSpec((tm, HGRN_W), row),
                  _layer_block(w, layer),
                  pl.BlockSpec((1, D_MODEL), const)],
        out_specs=pl.BlockSpec((tm, D_MODEL), row),
        out_shape=jax.ShapeDtypeStruct((n, D_MODEL), F32),
        compiler_params=pltpu.CompilerParams(dimension_semantics=("parallel",),
                                             vmem_limit_bytes=VMEM_LIMIT),
        name="outproj",
    )(xf, ya, yb, yc, w, nw)


def _block_mean_matrix(heads, dv):
    m = np.kron(np.eye(heads, dtype=np.float32), np.full((dv, dv), 1.0 / dv, np.float32))
    return jnp.asarray(m, BF16)


def kernel(x, norm_pre, norm_post, w_in, w_out, gla_wa2_fwd, gla_ba_fwd, gla_wa2_bwd, gla_ba_bwd, gla_norm,
           diff_lq1, diff_lk1, diff_lq2, diff_lk2, diff_norm, hgrn_lb_logits, hgrn_norm):
    batch, seq_len, d_model = x.shape
    assert d_model == D_MODEL and w_in.shape == (DEPTH, D_MODEL, IN_W)
    assert seq_len % ROW_TILE == 0 and seq_len % Q_TILE == 0 and seq_len % SCAN_TILE == 0
    assert seq_len % KEY_CHUNK == 0 and (batch * seq_len) % OUT_ROW_TILE == 0
    n = batch * seq_len
    xf = x.reshape(n, D_MODEL)

    w_perm = w_in.astype(BF16)
    wa2 = jnp.zeros((DEPTH, A_PAD, 2 * GLA_QK_W), F32)
    wa2 = wa2.at[:, :GLA_RANK, :GLA_QK_W].set(gla_wa2_fwd)
    wa2 = wa2.at[:, GLA_RANK:2 * GLA_RANK, GLA_QK_W:].set(gla_wa2_bwd).astype(BF16)
    ba = jnp.concatenate([gla_ba_fwd, gla_ba_bwd], axis=-1)
    w_out_b = w_out.astype(BF16)

    inv_freq = ROPE_THETA ** (-jnp.arange(0, DIFF_HD, 2, dtype=F32) / DIFF_HD)
    ang = jnp.arange(seq_len, dtype=jnp.int32).astype(F32)[:, None] * inv_freq[None, :]
    cos_t = jnp.tile(jnp.cos(ang), (1, 2 * LANES // DIFF_HD))
    sin_t = jnp.tile(jnp.concatenate([-jnp.sin(ang), jnp.sin(ang)], axis=-1), (1, LANES // DIFF_HD))

    tri = _tri_matrices(SCAN_TILE)
    bd_gla = _block_mean_matrix(GLA_HEADS, GLA_DV)
    bd_hgrn = _block_mean_matrix(HGRN_HEADS, HGRN_DV)

    prev = None
    for layer in range(DEPTH):
        lambda_init = 0.8 - 0.6 * math.exp(-0.3 * layer)
        outs = _inproj(xf, norm_pre[layer][None, :], w_perm, wa2, ba[layer][None, :],
                       cos_t, sin_t, seq_len, layer, prev)
        if prev is not None:
            xf, outs = outs[0], outs[1:]
        (aq, ak, av, alg, dq, dk, dv, hq, hz, hi, g) = outs
        yb = _attention(dq, dk, dv, g, diff_norm[layer][None, :], diff_lq1[layer][None, :],
                        diff_lk1[layer][None, :], diff_lq2[layer][None, :], diff_lk2[layer][None, :],
                        batch, seq_len, lambda_init)
        ya, yc = _scans(aq, ak, alg, av, hq, hz, hi, g, gla_norm[layer][None, :], hgrn_norm[layer][None, :],
                        hgrn_lb_logits, tri, bd_gla, bd_hgrn, batch, seq_len, layer)
        prev = (ya, yb, yc, w_out_b, norm_post[layer][None, :])
    xf = _outproj(xf, *prev, DEPTH - 1)
    return xf.reshape(batch, seq_len, D_MODEL)
```

```python
import functools
import math

import numpy as np
import jax
import jax.numpy as jnp
from jax import lax
from jax.experimental import pallas as pl
from jax.experimental.pallas import tpu as pltpu

F32 = jnp.float32
BF16 = jnp.bfloat16

D_MODEL = 1024
DEPTH = 4
GLA_HEADS, GLA_DK, GLA_DV, GLA_RANK = 4, 32, 64, 16
GLA_GATE_TEMP = 16.0
DIFF_HEADS, DIFF_HD = 4, 64
DIFF_DV = 2 * DIFF_HD
HGRN_HEADS, HGRN_DK, HGRN_DV = 4, 64, 64
ROPE_THETA = 10000.0
EPS = 1e-6
F_FLOOR = 1e-30
LOG2E = math.log2(math.e)

GLA_QK_W = GLA_HEADS * GLA_DK
GLA_W = GLA_HEADS * GLA_DV
DIFF_QK_W = DIFF_HEADS * 2 * DIFF_HD
DIFF_W = DIFF_HEADS * DIFF_DV
HGRN_QK_W = HGRN_HEADS * HGRN_DK
HGRN_W = HGRN_HEADS * HGRN_DV
MIX_W = GLA_W + DIFF_W + HGRN_W
IN_W = 2 * GLA_QK_W + 2 * GLA_W + 2 * GLA_RANK + 2 * DIFF_QK_W + 2 * DIFF_W + 3 * HGRN_QK_W + 2 * HGRN_W

LANES = 128
A_PAD = LANES
C_GLA = 0
C_DIFF = C_GLA + 2 * GLA_QK_W + 2 * GLA_W
C_HGRN = C_DIFF + 2 * DIFF_QK_W + 2 * DIFF_W
C_A = C_HGRN + 3 * HGRN_QK_W + 2 * HGRN_W
IN_WP = C_A + A_PAD

VMEM_LIMIT = 56 * 1024 * 1024

ROW_TILE = 512
OUT_ROW_TILE = 512
Q_TILE = 256
SCAN_TILE = 128


def _dot(a, b):
    return jnp.dot(a, b, preferred_element_type=F32)


def _dot_nt(a, b):
    return lax.dot_general(a, b, (((1,), (1,)), ((), ())), preferred_element_type=F32)


def _dot_tn(a, b):
    return lax.dot_general(a, b, (((0,), (0,)), ((), ())), preferred_element_type=F32)


def _split_bf16(x):
    hi = x.astype(BF16)
    lo = (x - hi.astype(F32)).astype(BF16)
    return hi, lo


def _sigmoid(x):
    return 1.0 / (1.0 + jnp.exp(-x))


def _silu(x):
    return x * _sigmoid(x)


def _inproj_kernel(x_ref, *refs):
    _inproj_body(x_ref[...], *refs)


def _inproj_body(x, nw_ref, w_raw_ref, wa2_ref, ba_ref, cos_ref, sin_ref,
                 aq_ref, ak_ref, av_ref, alg_ref, dq_ref, dk_ref, dv_ref,
                 hq_ref, hz_ref, hi_ref, g_ref, w_ref):
    @pl.when(pl.program_id(0) == 0)
    def _():
        a0 = 2 * GLA_QK_W + 2 * GLA_W
        w_ref[:, 0:a0] = w_raw_ref[:, 0:a0]
        w_ref[:, a0:C_A] = w_raw_ref[:, a0 + 2 * GLA_RANK:IN_W]
        w_ref[:, C_A:] = jnp.concatenate(
            [w_raw_ref[:, a0:a0 + 2 * GLA_RANK], jnp.zeros((D_MODEL, A_PAD - 2 * GLA_RANK), BF16)], axis=1)

    h = x * lax.rsqrt(jnp.mean(x * x, axis=-1, keepdims=True) + EPS) * nw_ref[...]
    hb = h.astype(BF16)

    def proj(c0, width):
        return _dot(hb, w_ref[:, c0:c0 + width])

    p = proj(C_GLA, 2 * GLA_QK_W + 2 * GLA_W)
    aq_ref[...] = (p[:, :GLA_QK_W] * (GLA_DK ** -0.5)).astype(BF16)
    ak_ref[...] = p[:, GLA_QK_W:2 * GLA_QK_W].astype(BF16)
    av_ref[...] = p[:, 2 * GLA_QK_W:2 * GLA_QK_W + GLA_W].astype(BF16)
    g_ref[:, 0:GLA_W] = p[:, 2 * GLA_QK_W + GLA_W:]
    a = proj(C_A, A_PAD)
    zz = _dot(a.astype(BF16), wa2_ref[...]) + ba_ref[...]
    alg_ref[...] = (jnp.minimum(zz, 0.0) - jnp.log1p(jnp.exp(-jnp.abs(zz)))) * (LOG2E / GLA_GATE_TEMP)

    cos = cos_ref[...]
    sin = sin_ref[...]
    lane = lax.broadcasted_iota(jnp.int32, (1, LANES), 1)
    first_half = (lane % DIFF_HD) < (DIFF_HD // 2)

    def rope_store(c0, out_ref, scale):
        pq = proj(c0, DIFF_QK_W)
        for j in range(DIFF_QK_W // LANES):
            xs = pq[:, j * LANES:(j + 1) * LANES]
            partner = jnp.where(first_half,
                                pltpu.roll(xs, LANES - DIFF_HD // 2, 1),
                                pltpu.roll(xs, DIFF_HD // 2, 1))
            r = xs * cos + partner * sin
            if scale != 1.0:
                r = r * scale
            out_ref[:, j * LANES:(j + 1) * LANES] = r.astype(BF16)

    rope_store(C_DIFF, dq_ref, DIFF_HD ** -0.5 * LOG2E)
    rope_store(C_DIFF + DIFF_QK_W, dk_ref, 1.0)
    p = proj(C_DIFF + 2 * DIFF_QK_W, 2 * DIFF_W)
    dv_ref[...] = p[:, :DIFF_W].astype(BF16)
    g_ref[:, GLA_W:GLA_W + DIFF_W] = p[:, DIFF_W:]

    p = proj(C_HGRN, 3 * HGRN_QK_W + 2 * HGRN_W)
    hq_ref[...] = p[:, :HGRN_QK_W].astype(BF16)
    hz_ref[...] = p[:, HGRN_QK_W:3 * HGRN_QK_W]
    hi_ref[...] = p[:, 3 * HGRN_QK_W:3 * HGRN_QK_W + HGRN_W].astype(BF16)
    g_ref[:, GLA_W + DIFF_W:] = p[:, 3 * HGRN_QK_W + HGRN_W:]


def _layer_block(arr, layer):
    return pl.BlockSpec((None,) + arr.shape[1:], lambda i: (layer, 0, 0), pipeline_mode=pl.Buffered(1))


def _inproj(xf, nw, w, wa2, ba, cos_t, sin_t, seq_len, layer, prev=None):
    n = xf.shape[0]
    tm = ROW_TILE
    n_pos_tiles = seq_len // tm
    row = lambda i: (i, 0)
    const = lambda i: (0, 0)
    pos = lambda i: (i % n_pos_tiles, 0)
    widths = [(GLA_QK_W, BF16), (GLA_QK_W, BF16), (GLA_W, BF16), (2 * GLA_QK_W, F32),
              (DIFF_QK_W, BF16), (DIFF_QK_W, BF16), (DIFF_W, BF16),
              (HGRN_QK_W, BF16), (2 * HGRN_QK_W, F32), (HGRN_W, BF16), (MIX_W, F32)]
    in_specs = [pl.BlockSpec((1, D_MODEL), const),
                _layer_block(w, layer),
                _layer_block(wa2, layer),
                pl.BlockSpec((1, 2 * GLA_QK_W), const),
                pl.BlockSpec((tm, LANES), pos),
                pl.BlockSpec((tm, LANES), pos)]
    operands = (nw, w, wa2, ba, cos_t, sin_t)
    if prev is None:
        body, name = _inproj_kernel, "inproj"
        lead_specs, lead = [pl.BlockSpec((tm, D_MODEL), row)], (xf,)
    else:
        body, name = _out_in_kernel, "outproj_inproj"
        widths = [(D_MODEL, F32)] + widths
        lead_specs = [pl.BlockSpec((tm, D_MODEL), row),
                      pl.BlockSpec((tm, GLA_W), row),
                      pl.BlockSpec((tm, DIFF_W), row),
                      pl.BlockSpec((tm, HGRN_W), row),
                      _layer_block(prev[3], layer - 1),
                      pl.BlockSpec((1, D_MODEL), const)]
        lead = (xf,) + tuple(prev)
    return pl.pallas_call(
        body,
        grid=(n // tm,),
        in_specs=lead_specs + in_specs,
        out_specs=[pl.BlockSpec((tm, wd), row) for wd, _ in widths],
        out_shape=[jax.ShapeDtypeStruct((n, wd), dt) for wd, dt in widths],
        scratch_shapes=[pltpu.VMEM((D_MODEL, IN_WP), BF16)],
        compiler_params=pltpu.CompilerParams(dimension_semantics=("arbitrary",),
                                             vmem_limit_bytes=VMEM_LIMIT),
        name=name,
    )(*lead, *operands)


HEADS_PER_STEP = 2
ONES_ROWS = 16
KEY_CHUNK = 512
SCORE_LOOKAHEAD = 6


def _attn_kernel(q_ref, k_ref, v_ref, g_ref, nw_ref, lq1_ref, lk1_ref, lq2_ref, lk2_ref, o_ref, vt_scr,
                 *, lambda_init):
    seq_len = k_ref.shape[0]
    head_cols = [slice(hd * DIFF_DV, (hd + 1) * DIFF_DV) for hd in range(HEADS_PER_STEP)]
    for hd, cols in enumerate(head_cols):
        vt_scr[hd, 0:DIFF_DV, :] = v_ref[:, cols].astype(F32).T.astype(BF16)
        vt_scr[hd, DIFF_DV:, :] = jnp.ones((ONES_ROWS, seq_len), BF16)

    lam = (jnp.exp(jnp.sum(lq1_ref[...] * lk1_ref[...], axis=-1, keepdims=True))
           - jnp.exp(jnp.sum(lq2_ref[...] * lk2_ref[...], axis=-1, keepdims=True)) + lambda_init)
    lane = lax.broadcasted_iota(jnp.int32, (1, 2 * DIFF_HD), 1)
    nk = seq_len // KEY_CHUNK
    items = [(hd, qi, c, j) for hd in range(HEADS_PER_STEP) for qi in range(seq_len // Q_TILE)
             for c in range(2) for j in range(nk)]
    qmap_cache = {}

    def qmap(hd, qi, c):
        if (hd, qi, c) not in qmap_cache:
            q = q_ref[qi * Q_TILE:(qi + 1) * Q_TILE, head_cols[hd]]
            qmap_cache[(hd, qi, c)] = jnp.where((lane // DIFF_HD) == c, q, jnp.zeros_like(q))
        return qmap_cache[(hd, qi, c)]

    def scores(item):
        hd, qi, c, j = item
        return _dot_nt(k_ref[j * KEY_CHUNK:(j + 1) * KEY_CHUNK, head_cols[hd]], qmap(hd, qi, c))

    ahead = [scores(it) for it in items[:SCORE_LOOKAHEAD]]
    parts, maxes, map_out = [], [], []
    for n, (hd, qi, c, j) in enumerate(items):
        st = ahead.pop(0)
        if n + SCORE_LOOKAHEAD < len(items):
            ahead.append(scores(items[n + SCORE_LOOKAHEAD]))
        m = jnp.max(st, axis=0, keepdims=True)
        e = jnp.exp2(st - m).astype(BF16)
        parts.append(_dot(vt_scr[hd, :, j * KEY_CHUNK:(j + 1) * KEY_CHUNK], e))
        maxes.append(m)
        if j < nk - 1:
            continue
        m_all = functools.reduce(jnp.maximum, maxes)
        tot = parts[0] * jnp.exp2(maxes[0] - m_all)
        for part, mj in zip(parts[1:], maxes[1:]):
            tot = tot + part * jnp.exp2(mj - m_all)
        map_out.append(tot[0:DIFF_DV, :] * (1.0 / tot[DIFF_DV:DIFF_DV + 1, :]))
        parts, maxes = [], []
        if c == 0:
            continue
        rows = slice(qi * Q_TILE, (qi + 1) * Q_TILE)
        o = (map_out[0] - lam * map_out[1]).T
        map_out = []
        o = o * lax.rsqrt(jnp.mean(o * o, axis=-1, keepdims=True) + EPS) * nw_ref[...] * (1.0 - lambda_init)
        o_ref[rows, head_cols[hd]] = (o * _silu(g_ref[rows, head_cols[hd]])).astype(BF16)


def _attention(dq, dk, dv, g, nw, lq1, lk1, lq2, lk2, batch, seq_len, lambda_init):
    n = dq.shape[0]
    wd = HEADS_PER_STEP * DIFF_DV
    assert DIFF_HEADS % HEADS_PER_STEP == 0 and GLA_W % wd == 0 and 2 * DIFF_HD == DIFF_DV
    bh = lambda b, h: (b, h)
    gmap = lambda b, h: (b, GLA_W // wd + h)
    const = lambda b, h: (0, 0)
    small = pl.BlockSpec((1, DIFF_HD), const)
    return pl.pallas_call(
        functools.partial(_attn_kernel, lambda_init=lambda_init),
        grid=(batch, DIFF_HEADS // HEADS_PER_STEP),
        in_specs=[pl.BlockSpec((seq_len, wd), bh),
                  pl.BlockSpec((seq_len, wd), bh),
                  pl.BlockSpec((seq_len, wd), bh),
                  pl.BlockSpec((seq_len, wd), gmap),
                  pl.BlockSpec((1, DIFF_DV), const),
                  small, small, small, small],
        out_specs=pl.BlockSpec((seq_len, wd), bh),
        out_shape=jax.ShapeDtypeStruct((n, DIFF_W), BF16),
        scratch_shapes=[pltpu.VMEM((HEADS_PER_STEP, DIFF_DV + ONES_ROWS, seq_len), BF16)],
        compiler_params=pltpu.CompilerParams(dimension_semantics=("parallel", "parallel"),
                                             vmem_limit_bytes=VMEM_LIMIT),
        name="diff_attn",
    )(dq, dk, dv, g, nw, lq1, lk1, lq2, lk2)


def _scan_levels(tile):
    nlev = int(math.log2(tile))
    assert 1 << nlev == tile
    return [tile >> (j + 1) for j in range(nlev)]


def _tri_matrices(tile):
    idx = np.arange(tile)
    tril = (idx[None, :] <= idx[:, None]).astype(np.float32)
    return jnp.asarray(tril, BF16), jnp.asarray(tril.T, BF16)


def _interleave(generators):
    live = list(generators)
    while live:
        for gen in list(live):
            try:
                next(gen)
            except StopIteration:
                live.remove(gen)


def _run_scans(scans, n_tiles):
    for init, _, _ in scans:
        init()

    def fwd(i, carry):
        _interleave([steps(i) for _, steps, _ in scans])
        return carry

    lax.fori_loop(0, n_tiles, fwd, 0, unroll=2)

    def bwd(n, carry):
        _interleave([steps(n_tiles - 1 - n) for _, _, steps in scans])
        return carry

    lax.fori_loop(0, n_tiles, bwd, 0, unroll=8)


def _make_scan(load_tile, v_ref, g_ref, nw, tril_ref, triu_ref, bd_ref, y_ref,
               pf_scr, sb_scr, o_scr, qb_scr, kb_scr, totb_scr, stf_scr, stb_scr,
               *, tile, heads, dk, dv):
    t_ = tile
    levels = _scan_levels(t_)
    w = heads * dk
    n_vt = heads * dv // LANES
    hpv = LANES // dv
    sub8 = lax.broadcasted_iota(jnp.int32, (8, 1), 0)
    lane = lax.broadcasted_iota(jnp.int32, (1, LANES), 1)
    xr = (lax.broadcasted_iota(jnp.int32, (t_, hpv * t_), 0)
          ^ (lax.broadcasted_iota(jnp.int32, (t_, hpv * t_), 1) & (t_ - 1)))
    vrow = lax.broadcasted_iota(jnp.int32, (LANES, LANES), 0)
    kcol = lax.broadcasted_iota(jnp.int32, (LANES, LANES), 1)

    def tile_info(p):
        first = p * hpv * dk
        cols = slice((first // LANES) * LANES, (first // LANES + 1) * LANES)
        off = first % LANES
        kmasks = [(lane >= off + r * dk) & (lane < off + (r + 1) * dk) for r in range(hpv)]
        vmasks = [(lane >= r * dv) & (lane < (r + 1) * dv) for r in range(hpv)]
        valid = (vrow // dv) == ((kcol - off) // dk)
        valid = valid & (kcol >= off) & (kcol < off + hpv * dk)
        return cols, kmasks, vmasks, valid

    def stack_heads(x, masks):
        return jnp.concatenate([jnp.where(mk, x, jnp.zeros_like(x)) for mk in masks], axis=0)

    def init():
        stf_scr[...] = jnp.zeros_like(stf_scr)
        stb_scr[...] = jnp.zeros_like(stb_scr)

    def fwd_steps(i):
        r0 = pl.multiple_of(i * t_, t_)
        rows = pl.ds(r0, t_)
        q, kf, kb, lgf, lgb = load_tile(rows)
        hi_f, lo_f = _split_bf16(lgf)
        hi_b, lo_b = _split_bf16(lgb)
        pp = _dot(tril_ref[...], jnp.concatenate([hi_f, lo_f], axis=1))
        ss = _dot(triu_ref[...], jnp.concatenate([hi_b, lo_b], axis=1))
        pf = pp[:, :w] + pp[:, w:]
        sb = ss[:, :w] + ss[:, w:]
        pf_scr[...] = pf
        sb_scr[...] = sb
        yield
        row_cache = {}

        def brow(scr, r):
            key = (id(scr), r)
            if key not in row_cache:
                row_cache[key] = jnp.broadcast_to(scr[r:r + 1, :], (8, w))
            return row_cache[key]

        def level_exponents(m):
            tgt, src = [], []
            for j in range(t_ // 8):
                r8 = slice(8 * j, 8 * j + 8)
                pf_t = pf_scr[r8, :]
                sb_t = sb_scr[r8, :]
                if m >= 8:
                    mid = (8 * j // (2 * m)) * 2 * m + m
                    gf, gb = brow(pf_scr, mid - 1), brow(sb_scr, mid)
                    if (8 * j // m) % 2 == 0:
                        tgt.append(sb_t - gb)
                        src.append(gf - pf_t)
                    else:
                        tgt.append(pf_t - gf)
                        src.append(gb - sb_t)
                    continue
                odd = (sub8 & m) != 0
                if m == 1:
                    tgt.append(jnp.where(odd, lgf[r8], lgb[r8]))
                    continue
                if m == 4:
                    gf, gb = brow(pf_scr, 8 * j + 3), brow(sb_scr, 8 * j + 4)
                else:
                    low = sub8 < 4
                    gf = jnp.where(low, brow(pf_scr, 8 * j + 1), brow(pf_scr, 8 * j + 5))
                    gb = jnp.where(low, brow(sb_scr, 8 * j + 2), brow(sb_scr, 8 * j + 6))
                d_f = pf_t - gf
                d_b = sb_t - gb
                tgt.append(jnp.where(odd, d_f, d_b))
                src.append(-jnp.where(odd, d_b, d_f))
            return tgt, src

        def pow2_16(pieces):
            return jnp.exp2(jnp.concatenate(pieces, axis=0)).astype(BF16)

        q16 = q.astype(BF16)
        kf16 = kf.astype(BF16)
        kb16 = kb.astype(BF16)
        row = lax.broadcasted_iota(jnp.int32, (t_, 1), 0)
        info = [tile_info(p) for p in range(n_vt)]
        acc = [None] * n_vt
        for li, m in enumerate(levels):
            tgt, src = level_exponents(m)
            lhs = q16 * pow2_16(tgt)
            if m >= 16:
                kk = jnp.concatenate([(kb16 if (r // m) % 2 else kf16)[r:r + m] for r in range(0, t_, m)], axis=0)
            else:
                kk = jnp.where((row & m) != 0, kb16, kf16)
            rhs = kk if m == 1 else kk * pow2_16(src)
            yield
            for p, (cols, kmasks, _, _) in enumerate(info):
                a = _dot_nt(lhs[:, cols], stack_heads(rhs[:, cols], kmasks))
                acc[p] = a if li == 0 else jnp.where(xr < 2 * m, a, acc[p])
                yield
        ksum16 = (kf + kb).astype(BF16)
        for p, (cols, kmasks, _, _) in enumerate(info):
            acc[p] = jnp.where(xr == 0, _dot_nt(q16[:, cols], stack_heads(ksum16[:, cols], kmasks)), acc[p])
        yield
        totf = pf_scr[t_ - 1:t_, :]
        totb = sb_scr[0:1, :]
        qf = q16 * jnp.exp2(pf).astype(BF16)
        kfd = kf16 * jnp.exp2(totf - pf).astype(BF16)
        yield
        qb_scr[rows, :] = q16 * jnp.exp2(sb).astype(BF16)
        kb_scr[rows, :] = kb16 * jnp.exp2(totb - sb).astype(BF16)
        dec_f = jnp.exp2(totf)
        totb_scr[i] = jnp.broadcast_to(jnp.exp2(totb), (8, w))
        yield

        v = v_ref[rows, :]
        outs = []
        for p, (cols, _, vmasks, valid) in enumerate(info):
            v_p = v[:, p * LANES:(p + 1) * LANES]
            st = stf_scr[p]
            outs.append(_dot(acc[p].astype(BF16), stack_heads(v_p, vmasks))
                        + _dot_nt(qf[:, cols], st.astype(BF16)))
            stf_scr[p] = jnp.where(valid, st * dec_f[:, cols] + _dot_tn(v_p, kfd[:, cols]), 0.0)
            yield
        o_scr[rows, :] = jnp.concatenate(outs, axis=1)

    def bwd_steps(i):
        r0 = pl.multiple_of(i * t_, t_)
        rows = pl.ds(r0, t_)
        v = v_ref[rows, :]
        qb = qb_scr[rows, :]
        kbd = kb_scr[rows, :]
        dec_b = totb_scr[i][0:1, :]
        outs = []
        for p in range(n_vt):
            cols, _, _, valid = tile_info(p)
            st = stb_scr[p]
            outs.append(_dot_nt(qb[:, cols], st.astype(BF16)))
            stb_scr[p] = jnp.where(valid, st * dec_b[:, cols] + _dot_tn(v[:, p * LANES:(p + 1) * LANES],
                                                                       kbd[:, cols]), 0.0)
            yield
        o = o_scr[rows, :] + jnp.concatenate(outs, axis=1)
        hi, lo = _split_bf16(o * o)
        bd = bd_ref[...]
        ms = _dot(hi, bd) + _dot(lo, bd)
        yield
        y = o * lax.rsqrt(ms + EPS) * nw * _silu(g_ref[rows, :])
        y_ref[rows, :] = y.astype(BF16)

    return init, fwd_steps, bwd_steps


N_SCAN_SCRATCH = 8


def _scans_kernel(aq_ref, ak_ref, alg_ref, av_ref, ag_ref, anw_ref, abd_ref,
                  hq_ref, hz_ref, lbl_ref, hv_ref, hg_ref, hnw_ref, hbd_ref, tril_ref, triu_ref,
                  ya_ref, yc_ref, *scratch, seq_len, tile, layer):
    def gla_tile(rows):
        k = ak_ref[rows, :]
        lg = alg_ref[rows, :]
        return aq_ref[rows, :], k, k, lg[:, :GLA_QK_W], lg[:, GLA_QK_W:]

    gla = _make_scan(gla_tile, av_ref, ag_ref, jnp.concatenate([anw_ref[...]] * GLA_HEADS, axis=1),
                     tril_ref, triu_ref, abd_ref, ya_ref, *scratch[:N_SCAN_SCRATCH],
                     tile=tile, heads=GLA_HEADS, dk=GLA_DK, dv=GLA_DV)

    logits = lbl_ref[...]
    e = jnp.exp(logits - jnp.max(logits, axis=0, keepdims=True))
    share = e / jnp.sum(e, axis=0, keepdims=True)
    lb = jnp.zeros((1, HGRN_QK_W), F32)
    for j in range(1, layer + 1):
        lb = lb + share[j:j + 1, :]

    def gates(z):
        t = jnp.exp(-jnp.abs(z))
        r = 1.0 / (1.0 + t)
        pos = z >= 0.0
        sig = jnp.where(pos, r, t * r)
        sig_neg = jnp.where(pos, t * r, r)
        f = lb + (1.0 - lb) * sig
        return jnp.log(jnp.maximum(f, F_FLOOR)) * LOG2E, (1.0 - lb) * sig_neg

    def hgrn_tile(rows):
        z = hz_ref[rows, :]
        lgf, kf = gates(z[:, :HGRN_QK_W])
        lgb, kb = gates(z[:, HGRN_QK_W:])
        return hq_ref[rows, :], kf, kb, lgf, lgb

    hgrn = _make_scan(hgrn_tile, hv_ref, hg_ref, jnp.concatenate([hnw_ref[...]] * HGRN_HEADS, axis=1),
                      tril_ref, triu_ref, hbd_ref, yc_ref, *scratch[N_SCAN_SCRATCH:],
                      tile=tile, heads=HGRN_HEADS, dk=HGRN_DK, dv=HGRN_DV)
    _run_scans([gla, hgrn], seq_len // tile)


def _scan_scratch(seq_len, tile, heads, dk, dv):
    w = heads * dk
    n_vt = heads * dv // LANES
    return [pltpu.VMEM((tile, w), F32),
            pltpu.VMEM((tile, w), F32),
            pltpu.VMEM((seq_len, heads * dv), F32),
            pltpu.VMEM((seq_len, w), BF16),
            pltpu.VMEM((seq_len, w), BF16),
            pltpu.VMEM((seq_len // tile, 8, w), F32),
            pltpu.VMEM((n_vt, LANES, LANES), F32),
            pltpu.VMEM((n_vt, LANES, LANES), F32)]


def _scans(aq, ak, alg, av, hq, hz, hi, g, gla_nw, hgrn_nw, lb_logits, tri, bd_gla, bd_hgrn,
           batch, seq_len, layer):
    n = batch * seq_len
    const = lambda b: (0, 0)
    seq = lambda wd, col=0: pl.BlockSpec((seq_len, wd), lambda b: (b, col))
    whole = lambda arr: pl.BlockSpec(arr.shape, const)
    scratch = (_scan_scratch(seq_len, SCAN_TILE, GLA_HEADS, GLA_DK, GLA_DV)
               + _scan_scratch(seq_len, SCAN_TILE, HGRN_HEADS, HGRN_DK, HGRN_DV))
    assert len(scratch) == 2 * N_SCAN_SCRATCH
    return pl.pallas_call(
        functools.partial(_scans_kernel, seq_len=seq_len, tile=SCAN_TILE, layer=layer),
        grid=(batch,),
        in_specs=[seq(GLA_QK_W), seq(GLA_QK_W), seq(2 * GLA_QK_W), seq(GLA_W), seq(GLA_W, 0),
                  whole(gla_nw), whole(bd_gla),
                  seq(HGRN_QK_W), seq(2 * HGRN_QK_W), whole(lb_logits), seq(HGRN_W),
                  seq(HGRN_W, (GLA_W + DIFF_W) // HGRN_W), whole(hgrn_nw), whole(bd_hgrn),
                  whole(tri[0]), whole(tri[1])],
        out_specs=[seq(GLA_W), seq(HGRN_W)],
        out_shape=[jax.ShapeDtypeStruct((n, GLA_W), BF16), jax.ShapeDtypeStruct((n, HGRN_W), BF16)],
        scratch_shapes=scratch,
        compiler_params=pltpu.CompilerParams(dimension_semantics=("parallel",),
                                             vmem_limit_bytes=VMEM_LIMIT),
        name="scans",
    )(aq, ak, alg, av, g, gla_nw, bd_gla, hq, hz, lb_logits, hi, g, hgrn_nw, bd_hgrn, tri[0], tri[1])


def _outproj_value(x_ref, ya_ref, yb_ref, yc_ref, w_ref, nw_ref):
    y = (_dot(ya_ref[...], w_ref[0:GLA_W, :])
         + _dot(yb_ref[...], w_ref[GLA_W:GLA_W + DIFF_W, :])
         + _dot(yc_ref[...], w_ref[GLA_W + DIFF_W:, :]))
    y = y * lax.rsqrt(jnp.mean(y * y, axis=-1, keepdims=True) + EPS) * nw_ref[...]
    return x_ref[...] + y


def _outproj_kernel(x_ref, ya_ref, yb_ref, yc_ref, w_ref, nw_ref, o_ref):
    o_ref[...] = _outproj_value(x_ref, ya_ref, yb_ref, yc_ref, w_ref, nw_ref)


def _out_in_kernel(x_ref, ya_ref, yb_ref, yc_ref, wout_ref, npost_ref,
                   npre_ref, w_ref, wa2_ref, ba_ref, cos_ref, sin_ref, xnew_ref, *outs):
    x = _outproj_value(x_ref, ya_ref, yb_ref, yc_ref, wout_ref, npost_ref)
    xnew_ref[...] = x
    _inproj_body(x, npre_ref, w_ref, wa2_ref, ba_ref, cos_ref, sin_ref, *outs)


def _outproj(xf, ya, yb, yc, w, nw, layer):
    n = xf.shape[0]
    tm = OUT_ROW_TILE
    row = lambda i: (i, 0)
    const = lambda i: (0, 0)
    return pl.pallas_call(
        _outproj_kernel,
        grid=(n // tm,),
        in_specs=[pl.BlockSpec((tm, D_MODEL), row),
                  pl.BlockSpec((tm, GLA_W), row),
                  pl.BlockSpec((tm, DIFF_W), row),
                  pl.BlockSpec((tm, HGRN_W), row),
                  _layer_block(w, layer),
                  pl.BlockSpec((1, D_MODEL), const)],
        out_specs=pl.BlockSpec((tm, D_MODEL), row),
        out_shape=jax.ShapeDtypeStruct((n, D_MODEL), F32),
        compiler_params=pltpu.CompilerParams(dimension_semantics=("parallel",),
                                             vmem_limit_bytes=VMEM_LIMIT),
        name="outproj",
    )(xf, ya, yb, yc, w, nw)


def _block_mean_matrix(heads, dv):
    m = np.kron(np.eye(heads, dtype=np.float32), np.full((dv, dv), 1.0 / dv, np.float32))
    return jnp.asarray(m, BF16)


def kernel(x, norm_pre, norm_post, w_in, w_out, gla_wa2_fwd, gla_ba_fwd, gla_wa2_bwd, gla_ba_bwd, gla_norm,
           diff_lq1, diff_lk1, diff_lq2, diff_lk2, diff_norm, hgrn_lb_logits, hgrn_norm):
    batch, seq_len, d_model = x.shape
    assert d_model == D_MODEL and w_in.shape == (DEPTH, D_MODEL, IN_W)
    assert seq_len % ROW_TILE == 0 and seq_len % Q_TILE == 0 and seq_len % SCAN_TILE == 0
    assert seq_len % KEY_CHUNK == 0 and (batch * seq_len) % OUT_ROW_TILE == 0
    n = batch * seq_len
    xf = x.reshape(n, D_MODEL)

    w_perm = jnp.pad(w_in, ((0, 0), (0, 0), (0, IN_WP - IN_W))).astype(BF16)
    wa2 = jnp.zeros((DEPTH, A_PAD, 2 * GLA_QK_W), F32)
    wa2 = wa2.at[:, :GLA_RANK, :GLA_QK_W].set(gla_wa2_fwd)
    wa2 = wa2.at[:, GLA_RANK:2 * GLA_RANK, GLA_QK_W:].set(gla_wa2_bwd).astype(BF16)
    ba = jnp.concatenate([gla_ba_fwd, gla_ba_bwd], axis=-1)
    w_out_b = w_out.astype(BF16)

    inv_freq = ROPE_THETA ** (-jnp.arange(0, DIFF_HD, 2, dtype=F32) / DIFF_HD)
    ang = jnp.arange(seq_len, dtype=jnp.int32).astype(F32)[:, None] * inv_freq[None, :]
    cos_t = jnp.tile(jnp.cos(ang), (1, 2 * LANES // DIFF_HD))
    sin_t = jnp.tile(jnp.concatenate([-jnp.sin(ang), jnp.sin(ang)], axis=-1), (1, LANES // DIFF_HD))

    tri = _tri_matrices(SCAN_TILE)
    bd_gla = _block_mean_matrix(GLA_HEADS, GLA_DV)
    bd_hgrn = _block_mean_matrix(HGRN_HEADS, HGRN_DV)

    prev = None
    for layer in range(DEPTH):
        lambda_init = 0.8 - 0.6 * math.exp(-0.3 * layer)
        outs = _inproj(xf, norm_pre[layer][None, :], w_perm, wa2, ba[layer][None, :],
                       cos_t, sin_t, seq_len, layer, prev)
        if prev is not None:
            xf, outs = outs[0], outs[1:]
        (aq, ak, av, alg, dq, dk, dv, hq, hz, hi, g) = outs
        yb = _attention(dq, dk, dv, g, diff_norm[layer][None, :], diff_lq1[layer][None, :],
                        diff_lk1[layer][None, :], diff_lq2[layer][None, :], diff_lk2[layer][None, :],
                        batch, seq_len, lambda_init)
        ya, yc = _scans(aq, ak, alg, av, hq, hz, hi, g, gla_norm[layer][None, :], hgrn_norm[layer][None, :],
                        hgrn_lb_logits, tri, bd_gla, bd_hgrn, batch, seq_len, layer)
        prev = (ya, yb, yc, w_out_b, norm_post[layer][None, :])
    xf = _outproj(xf, *prev, DEPTH - 1)
    return xf.reshape(batch, seq_len, D_MODEL)
```
